```python
import math
import jax, jax.numpy as jnp
from jax import lax
import numpy as np

D_MODEL = 1024
BATCH = 4
SEQ = 4096
DEPTH = 4
DEC_BATCH = 128
DEC_SEQ = 8
PAST_LEN = 8192
PAGE_SIZE = 128

N_AB_LAYERS = (DEPTH + 1) // 2
N_C_LAYERS = DEPTH // 2
M_HEADS = 4
M_QK_DIM = 64
M_V_DIM = 128
M_CONV = 4
M_CHUNK = 128
A_HEADS = 4
A_KV_HEADS = 2
A_HEAD_DIM = 128
IDX_HEADS = 4
IDX_DIM = 64
DSA_TOPK = 256
IDX_SCALE = (IDX_HEADS * IDX_DIM) ** -0.5
Q_BLOCK = 128
C_HEADS = 8
Q_LORA = 256
KV_LORA = 128
NOPE_DIM = 128
ROPE_DIM = 64
C_V_DIM = 128
MLA_SCALE = (NOPE_DIM + ROPE_DIM) ** -0.5
C_IN = Q_LORA + KV_LORA + ROPE_DIM
D_FF = 2816
FFN_CONV = 3
ROPE_THETA = 10000.0
EPS = 1e-6
F32 = jnp.float32
AB_WIDTHS = (2 * M_HEADS * M_QK_DIM, M_HEADS * M_V_DIM, M_HEADS * M_V_DIM, M_HEADS, M_HEADS,
             A_HEADS * A_HEAD_DIM, A_KV_HEADS * A_HEAD_DIM, A_KV_HEADS * A_HEAD_DIM,
             IDX_HEADS * IDX_DIM, IDX_DIM, IDX_HEADS)
AB_IN = sum(AB_WIDTHS)
AB_CUTS = tuple(int(c) for c in np.cumsum(AB_WIDTHS)[:-1])
AB_MIX = M_HEADS * M_V_DIM + A_HEADS * A_HEAD_DIM

kernel_name = 'hybrid_mlstm_dsa_mla_convffn_step'


def rmsnorm(x, g):
    xf = x.astype(F32)
    y = xf * lax.rsqrt(jnp.mean(xf * xf, axis=-1, keepdims=True) + EPS)
    return (y * g.astype(F32)).astype(x.dtype)


def rope(x, pos):
    d = x.shape[-1]
    half = d // 2
    inv = ROPE_THETA ** (-jnp.arange(half, dtype=F32) * (2.0 / d))
    ang = pos.astype(F32)[:, None] * inv[None, :]
    shp = (pos.shape[0],) + (1,) * (x.ndim - 3) + (half,)
    cos = jnp.cos(ang).reshape(shp)
    sin = jnp.sin(ang).reshape(shp)
    xf = x.astype(F32)
    x1, x2 = xf[..., :half], xf[..., half:]
    return jnp.concatenate([x1 * cos - x2 * sin, x2 * cos + x1 * sin], axis=-1).astype(x.dtype)


def causal_dwconv(x, buf, w):
    W = w.shape[0]
    T = x.shape[1]
    xp = jnp.concatenate([buf.astype(x.dtype), x], axis=1)
    y = xp[:, W - 1:W - 1 + T] * w[W - 1]
    for j in range(W - 1):
        y = y + xp[:, j:j + T] * w[j]
    return y, xp[:, T:]


def mlstm_chunkwise(q, k, v, ig, fg, C0, n0, m0):
    B, T, H, DK = q.shape
    DV = v.shape[-1]
    cl = math.gcd(T, M_CHUNK)
    nc = T // cl

    def heads_first(a):
        return jnp.transpose(a.astype(F32).reshape(B, nc, cl, H, -1), (1, 0, 3, 2, 4))

    def gates_first(a):
        return jnp.transpose(a.astype(F32).reshape(B, nc, cl, H), (1, 0, 3, 2))

    causal = jnp.tril(jnp.ones((cl, cl), dtype=bool))

    def step(carry, xs):
        C, n, m = carry
        qc, kc, vc, li, lf = xs
        b = jnp.cumsum(lf, axis=-1)
        inter = b + m[..., None]
        D = jnp.where(causal, b[..., :, None] - b[..., None, :] + li[..., None, :], -jnp.inf)
        m_t = jnp.maximum(inter, jnp.max(D, axis=-1))
        iw = jnp.exp(inter - m_t)
        s = jnp.einsum('bhtd,bhsd->bhts', qc, kc) * jnp.exp(D - m_t[..., None])
        num = iw[..., None] * jnp.einsum('bhtd,bhde->bhte', qc, C) + jnp.einsum('bhts,bhse->bhte', s, vc)
        den = iw * jnp.einsum('bhtd,bhd->bht', qc, n) + jnp.sum(s, axis=-1)
        h = num / jnp.maximum(jnp.abs(den), jnp.exp(-m_t))[..., None]
        m_new = m_t[..., -1]
        w_end = jnp.exp(b[..., -1:] - b + li - m_new[..., None])
        decay = jnp.exp(b[..., -1] + m - m_new)
        C_new = decay[..., None, None] * C + jnp.einsum('bhs,bhsd,bhse->bhde', w_end, kc, vc)
        n_new = decay[..., None] * n + jnp.einsum('bhs,bhsd->bhd', w_end, kc)
        return (C_new, n_new, m_new), h

    xs = (heads_first(q), heads_first(k), heads_first(v), gates_first(ig),
          gates_first(jax.nn.log_sigmoid(fg.astype(F32))))
    (C1, n1, m1), h = lax.scan(step, (C0.astype(F32), n0.astype(F32), m0.astype(F32)), xs)
    h = jnp.transpose(h, (1, 0, 3, 2, 4)).reshape(B, T, H, DV)
    return h, C1, n1, m1


def gather_dense(a, idx):
    return jax.vmap(lambda ab, ib: ab[ib])(a, idx)


def gather_pages(pool, layer, page_table):
    rows = pool[layer, page_table]
    B, NP = page_table.shape
    return rows.reshape(B, NP * PAGE_SIZE, *rows.shape[3:])


def gather_rows(pool, layer, page_table, new, pos):
    B = pos.shape[0]
    n_past = page_table.shape[1] * PAGE_SIZE
    pp = jnp.minimum(pos, n_past - 1)
    phys = jnp.take_along_axis(page_table, (pp // PAGE_SIZE).reshape(B, -1), axis=1).reshape(pos.shape)
    from_pool = pool[layer, phys, pp % PAGE_SIZE]
    from_new = gather_dense(new, jnp.clip(pos - n_past, 0, new.shape[1] - 1))
    in_past = (pos < n_past).reshape(pos.shape + (1,) * (from_new.ndim - pos.ndim))
    return jnp.where(in_past, from_pool, from_new)


def indexer_select(qi, wi, ki, qpos, kpos, topk):
    rel = jax.nn.relu(jnp.einsum('bthd,bsd->bths', qi, ki).astype(F32))
    score = jnp.einsum('bths,bth->bts', rel, wi.astype(F32) * IDX_SCALE)
    visible = kpos[None, None, :] <= qpos[None, :, None]
    _, idx = lax.top_k(jnp.where(visible, score, -jnp.inf), topk)
    valid = jnp.take(kpos, idx) <= qpos[None, :, None]
    return idx, valid


def sparse_attend(q, ks, vs, valid):
    B, T, H, D = q.shape
    G = ks.shape[3]
    qg = q.reshape(B, T, G, H // G, D)
    s = jnp.einsum('btgrd,btkgd->btgrk', qg, ks).astype(F32) * (D ** -0.5)
    s = jnp.where(valid[:, :, None, None, :], s, -jnp.inf)
    p = jax.nn.softmax(s, axis=-1).astype(vs.dtype)
    return jnp.einsum('btgrk,btkgd->btgrd', p, vs).reshape(B, T, H, D)


def dsa_attend(q, qi, wi, qpos, ki_all, kpos, topk, gather_kv):
    idx, valid = indexer_select(qi, wi, ki_all, qpos, kpos, topk)
    ks, vs = gather_kv(idx)
    return sparse_attend(q, ks, vs, valid)


def mla_attend(q_lat, q_rope, ckv, kr, qpos, kpos):
    s = (jnp.einsum('bthr,bsr->bhts', q_lat, ckv) + jnp.einsum('bthe,bse->bhts', q_rope, kr)).astype(F32) * MLA_SCALE
    s = jnp.where(kpos[None, None, None, :] <= qpos[None, None, :, None], s, -jnp.inf)
    p = jax.nn.softmax(s, axis=-1).astype(ckv.dtype)
    return jnp.einsum('bhts,bsr->bthr', p, ckv)


def map_query_blocks(fn, qpos, *qs):
    T = qpos.shape[0]
    qb = math.gcd(T, Q_BLOCK)
    nb = T // qb

    def to_blocks(a):
        return jnp.moveaxis(a.reshape(a.shape[0], nb, qb, *a.shape[2:]), 1, 0)

    blocks = (qpos.reshape(nb, qb),) + tuple(to_blocks(a) for a in qs)
    out = jnp.moveaxis(lax.map(lambda xs: fn(*xs), blocks), 0, 1)
    return out.reshape(out.shape[0], T, *out.shape[3:])


def ab_mixer(h, pos, w_in, w_mconv, b_ig, b_fg, g_mhead, w_out, past):
    B, T, _ = h.shape
    (qk_m, v_m, o_m, i_m, f_m, q_a, k_a, v_a, q_i, k_i, w_i) = jnp.split(h @ w_in, AB_CUTS, axis=-1)
    if past is None:
        conv_buf = jnp.zeros((B, M_CONV - 1, qk_m.shape[-1]), h.dtype)
        C0 = jnp.zeros((B, M_HEADS, M_QK_DIM, M_V_DIM), F32)
        n0 = jnp.zeros((B, M_HEADS, M_QK_DIM), F32)
        m0 = jnp.zeros((B, M_HEADS), F32)
    else:
        C0, n0, m0, conv_buf, pool_k, pool_v, pool_ki, layer, page_table = past
    qk_c, conv_new = causal_dwconv(qk_m, conv_buf, w_mconv)
    qk_c = jax.nn.silu(qk_c).reshape(B, T, 2, M_HEADS, M_QK_DIM)
    h_t, C1, n1, m1 = mlstm_chunkwise(qk_c[:, :, 0], qk_c[:, :, 1] * (M_QK_DIM ** -0.5),
                                      v_m.reshape(B, T, M_HEADS, M_V_DIM), i_m + b_ig, f_m + b_fg, C0, n0, m0)
    h_m = rmsnorm(h_t.astype(h.dtype), g_mhead) * jax.nn.sigmoid(o_m).reshape(B, T, M_HEADS, M_V_DIM)
    q_a = rope(q_a.reshape(B, T, A_HEADS, A_HEAD_DIM), pos)
    k_a = rope(k_a.reshape(B, T, A_KV_HEADS, A_HEAD_DIM), pos)
    v_a = v_a.reshape(B, T, A_KV_HEADS, A_HEAD_DIM)
    q_i = rope(q_i.reshape(B, T, IDX_HEADS, IDX_DIM), pos)
    k_i = rope(k_i, pos)
    if past is None:
        topk = min(DSA_TOPK, T // 4)
        gather_kv = lambda idx: (gather_dense(k_a, idx), gather_dense(v_a, idx))
        o_a = map_query_blocks(lambda qp, qb, qib, wib: dsa_attend(qb, qib, wib, qp, k_i, pos, topk, gather_kv),
                               pos, q_a, q_i, w_i)
    else:
        L = page_table.shape[1] * PAGE_SIZE + T
        ki_all = jnp.concatenate([gather_pages(pool_ki, layer, page_table), k_i], axis=1)
        topk = min(DSA_TOPK, L // 4)
        gather_kv = lambda idx: (gather_rows(pool_k, layer, page_table, k_a, idx),
                                 gather_rows(pool_v, layer, page_table, v_a, idx))
        o_a = dsa_attend(q_a, q_i, w_i, pos, ki_all, jnp.arange(L), topk, gather_kv)
    y = jnp.concatenate([h_m.reshape(B, T, -1), o_a.reshape(B, T, -1).astype(h_m.dtype)], axis=-1) @ w_out
    return y, (C1, n1, m1, conv_new, k_a, v_a, k_i)


def mla_mixer(h, pos, w_in, g_cq, g_ckv, w_uq, w_uk, w_uv, w_out, past):
    B, T, _ = h.shape
    cq, ckv, kr = jnp.split(h @ w_in, [Q_LORA, Q_LORA + KV_LORA], axis=-1)
    cq = rmsnorm(cq, g_cq)
    ckv = rmsnorm(ckv, g_ckv)
    kr = rope(kr, pos)
    q = (cq @ w_uq).reshape(B, T, C_HEADS, NOPE_DIM + ROPE_DIM)
    q_rope = rope(q[..., NOPE_DIM:], pos)
    q_lat = jnp.einsum('bthn,hnr->bthr', q[..., :NOPE_DIM], w_uk)
    if past is None:
        o_lat = map_query_blocks(lambda qp, ql, qr: mla_attend(ql, qr, ckv, kr, qp, pos), pos, q_lat, q_rope)
    else:
        pool_ckv, pool_kr, layer, page_table = past
        ckv_all = jnp.concatenate([gather_pages(pool_ckv, layer, page_table), ckv], axis=1)
        kr_all = jnp.concatenate([gather_pages(pool_kr, layer, page_table), kr], axis=1)
        o_lat = mla_attend(q_lat, q_rope, ckv_all, kr_all, pos, jnp.arange(ckv_all.shape[1]))
    o = jnp.einsum('bthr,hrv->bthv', o_lat, w_uv).reshape(B, T, -1) @ w_out
    return o, (ckv, kr)


def conv_ffn(h, w_up, w_conv, b_conv, w_down, buf):
    a, g = jnp.split(h @ w_up, 2, axis=-1)
    gc, buf_new = causal_dwconv(g, buf, w_conv)
    return (a * jax.nn.silu(gc + b_conv)) @ w_down, buf_new


def trunk(x, pos, weights, past):
    (g_attn, g_ffn, g_final, w_in_ab, w_mconv, b_igate, b_fgate, g_mhead, w_out_ab,
     w_in_mla, g_cq, g_ckv, w_uq, w_uk, w_uv, w_out_mla, w_up, w_fconv, b_fconv, w_down) = weights
    if past is not None:
        (st_C, st_n, st_m, st_mconv, c_k, c_v, c_ki, c_ckv, c_kr, st_fconv, page_table) = past
    ab_states, c_states, ffn_states = [], [], []
    B = x.shape[0]
    for l in range(DEPTH):
        j = l // 2
        h = rmsnorm(x, g_attn[l])
        if l % 2 == 0:
            p = None if past is None else (st_C[j], st_n[j], st_m[j], st_mconv[j], c_k, c_v, c_ki, j, page_table)
            y, s = ab_mixer(h, pos, w_in_ab[j], w_mconv[j], b_igate[j], b_fgate[j], g_mhead[j], w_out_ab[j], p)
            ab_states.append(s)
        else:
            p = None if past is None else (c_ckv, c_kr, j, page_table)
            y, s = mla_mixer(h, pos, w_in_mla[j], g_cq[j], g_ckv[j], w_uq[j], w_uk[j], w_uv[j], w_out_mla[j], p)
            c_states.append(s)
        x = x + y
        h = rmsnorm(x, g_ffn[l])
        buf = jnp.zeros((B, FFN_CONV - 1, D_FF), h.dtype) if past is None else st_fconv[l]
        y, b = conv_ffn(h, w_up[l], w_fconv[l], b_fconv[l], w_down[l], buf)
        ffn_states.append(b)
        x = x + y
    ab = [jnp.stack(s) for s in zip(*ab_states)]
    cc = [jnp.stack(s) for s in zip(*c_states)]
    return (rmsnorm(x, g_final), *ab, *cc, jnp.stack(ffn_states))


def setup_inputs(seed: int = 0) -> dict:
    key = jax.random.key(seed)
    ks = iter(jax.random.split(key, 48))
    nrm = lambda shape, scale: jax.random.normal(next(ks), shape, F32) * scale
    gain = lambda shape: 1.0 + nrm(shape, 0.01)
    n_pages = PAST_LEN // PAGE_SIZE
    n_pool = (DEC_BATCH * n_pages * 5) // 4
    page_table = jax.random.permutation(next(ks), n_pool)[:DEC_BATCH * n_pages].reshape(DEC_BATCH, n_pages).astype(jnp.int32)
    return {
        'x_prompt': nrm((BATCH, SEQ, D_MODEL), 1.0),
        'x_sample': nrm((DEC_BATCH, DEC_SEQ, D_MODEL), 1.0),
        'state_mlstm_C': nrm((N_AB_LAYERS, DEC_BATCH, M_HEADS, M_QK_DIM, M_V_DIM), 0.3),
        'state_mlstm_n': nrm((N_AB_LAYERS, DEC_BATCH, M_HEADS, M_QK_DIM), 0.3),
        'state_mlstm_m': nrm((N_AB_LAYERS, DEC_BATCH, M_HEADS), 1.0),
        'state_mlstm_conv': nrm((N_AB_LAYERS, DEC_BATCH, M_CONV - 1, 2 * M_HEADS * M_QK_DIM), 1.0),
        'cache_dsa_k': nrm((N_AB_LAYERS, n_pool, PAGE_SIZE, A_KV_HEADS, A_HEAD_DIM), 1.0),
        'cache_dsa_v': nrm((N_AB_LAYERS, n_pool, PAGE_SIZE, A_KV_HEADS, A_HEAD_DIM), 1.0),
        'cache_dsa_kidx': nrm((N_AB_LAYERS, n_pool, PAGE_SIZE, IDX_DIM), 1.0),
        'cache_mla_ckv': nrm((N_C_LAYERS, n_pool, PAGE_SIZE, KV_LORA), 1.0),
        'cache_mla_krope': nrm((N_C_LAYERS, n_pool, PAGE_SIZE, ROPE_DIM), 1.0),
        'state_ffn_conv': nrm((DEPTH, DEC_BATCH, FFN_CONV - 1, D_FF), 1.0),
        'page_table': page_table,
        'g_attn': gain((DEPTH, D_MODEL)),
        'g_ffn': gain((DEPTH, D_MODEL)),
        'g_final': gain((D_MODEL,)),
        'w_in_ab': nrm((N_AB_LAYERS, D_MODEL, AB_IN), D_MODEL ** -0.5),
        'w_mconv': nrm((N_AB_LAYERS, M_CONV, 2 * M_HEADS * M_QK_DIM), M_CONV ** -0.5),
        'b_igate': nrm((N_AB_LAYERS, M_HEADS), 0.1),
        'b_fgate': 3.0 + nrm((N_AB_LAYERS, M_HEADS), 0.1),
        'g_mhead': gain((N_AB_LAYERS, M_HEADS, M_V_DIM)),
        'w_out_ab': nrm((N_AB_LAYERS, AB_MIX, D_MODEL), AB_MIX ** -0.5),
        'w_in_mla': nrm((N_C_LAYERS, D_MODEL, C_IN), D_MODEL ** -0.5),
        'g_cq': gain((N_C_LAYERS, Q_LORA)),
        'g_ckv': gain((N_C_LAYERS, KV_LORA)),
        'w_uq': nrm((N_C_LAYERS, Q_LORA, C_HEADS * (NOPE_DIM + ROPE_DIM)), Q_LORA ** -0.5),
        'w_uk': nrm((N_C_LAYERS, C_HEADS, NOPE_DIM, KV_LORA), KV_LORA ** -0.5),
        'w_uv': nrm((N_C_LAYERS, C_HEADS, KV_LORA, C_V_DIM), KV_LORA ** -0.5),
        'w_out_mla': nrm((N_C_LAYERS, C_HEADS * C_V_DIM, D_MODEL), (C_HEADS * C_V_DIM) ** -0.5),
        'w_up': nrm((DEPTH, D_MODEL, 2 * D_FF), D_MODEL ** -0.5),
        'w_fconv': nrm((DEPTH, FFN_CONV, D_FF), FFN_CONV ** -0.5),
        'b_fconv': nrm((DEPTH, D_FF), 0.02),
        'w_down': nrm((DEPTH, D_FF, D_MODEL), D_FF ** -0.5),
    }


def reference(x_prompt, x_sample, state_mlstm_C, state_mlstm_n, state_mlstm_m, state_mlstm_conv,
              cache_dsa_k, cache_dsa_v, cache_dsa_kidx, cache_mla_ckv, cache_mla_krope, state_ffn_conv,
              page_table, g_attn, g_ffn, g_final, w_in_ab, w_mconv, b_igate, b_fgate, g_mhead, w_out_ab,
              w_in_mla, g_cq, g_ckv, w_uq, w_uk, w_uv, w_out_mla, w_up, w_fconv, b_fconv, w_down):
    weights = (g_attn, g_ffn, g_final, w_in_ab, w_mconv, b_igate, b_fgate, g_mhead, w_out_ab,
               w_in_mla, g_cq, g_ckv, w_uq, w_uk, w_uv, w_out_mla, w_up, w_fconv, b_fconv, w_down)
    pos_p = jnp.arange(x_prompt.shape[1])
    pos_s = page_table.shape[1] * PAGE_SIZE + jnp.arange(x_sample.shape[1])
    (y_prompt, p_C, p_n, p_m, p_mconv, p_k, p_v, p_ki, p_ckv, p_kr, p_fconv) = trunk(x_prompt, pos_p, weights, None)
    past = (state_mlstm_C, state_mlstm_n, state_mlstm_m, state_mlstm_conv, cache_dsa_k, cache_dsa_v,
            cache_dsa_kidx, cache_mla_ckv, cache_mla_krope, state_ffn_conv, page_table)
    (y_sample, s_C, s_n, s_m, s_mconv, s_k, s_v, s_ki, s_ckv, s_kr, s_fconv) = trunk(x_sample, pos_s, weights, past)
    return (y_prompt, y_sample, p_C, p_n, p_m, p_mconv, p_k, p_v, p_ki, p_ckv, p_kr, p_fconv,
            s_C, s_n, s_m, s_mconv, s_k, s_v, s_ki, s_ckv, s_kr, s_fconv)
```

```python
import functools
import math

import jax
import jax.numpy as jnp
import numpy as np
from jax import lax
from jax.experimental import pallas as pl
from jax.experimental.pallas import tpu as pltpu

F32 = jnp.float32
BF16 = jnp.bfloat16
I32 = jnp.int32

D_MODEL = 1024
DEPTH = 4
PAGE = 128
M_HEADS, M_QK, M_V, M_CONV, M_CHUNK = 4, 64, 128, 4, 128
A_HEADS, A_KV, A_DIM = 4, 2, 128
IDX_HEADS, IDX_DIM, DSA_TOPK = 4, 64, 256
IDX_SCALE = (IDX_HEADS * IDX_DIM) ** -0.5
C_HEADS, Q_LORA, KV_LORA, NOPE, ROPE_D, C_V = 8, 256, 128, 128, 64, 128
MLA_SCALE = (NOPE + ROPE_D) ** -0.5
D_FF, FFN_CONV = 2816, 3
ROPE_THETA = 10000.0
EPS = 1e-6
AB_WIDTHS = (512, 512, 512, 4, 4, 512, 256, 256, 256, 64, 4)
AB_PACKED = 2944
INT_MIN = -(2 ** 31)
NEG = -1e30
VMEM_LIMIT = 56 * 1024 * 1024

_NT = (((1,), (1,)), ((), ()))


def _cp(sem):
    return pltpu.CompilerParams(dimension_semantics=sem, vmem_limit_bytes=VMEM_LIMIT)


def _dot(a, b):
    return jnp.dot(a, b, preferred_element_type=F32)


def _dot_nt(a, b):
    return lax.dot_general(a, b, _NT, preferred_element_type=F32)


def _rms(x, g):
    return x * lax.rsqrt(jnp.mean(x * x, axis=-1, keepdims=True) + EPS) * g


def _rope_piece(p, cos, sin, d):
    if d == 128:
        rot = pltpu.roll(p, 64, axis=1)
    else:
        lane = lax.broadcasted_iota(I32, p.shape, 1)
        rot = jnp.where((lane & 63) < 32, pltpu.roll(p, 96, axis=1), pltpu.roll(p, 32, axis=1))
    return p * cos + rot * sin


def _rope_wide(x, cos, sin, d):
    n = x.shape[1] // 128
    return jnp.concatenate([_rope_piece(x[:, i * 128:(i + 1) * 128], cos, sin, d) for i in range(n)], axis=1)


def _float_key(s):
    bits = pltpu.bitcast(s, I32)
    key = bits ^ ((bits >> 31) & 0x7FFFFFFF)
    return jnp.where(bits == INT_MIN, 0, key)


def _topk_threshold(keys_ref, rows, n_chunks, cw, k, idx_bits):
    def count(pred):
        def body(c, acc):
            p = pred(keys_ref[c], c).astype(I32)
            part = p[:, 0:128]
            for u in range(1, cw // 128):
                part = part + p[:, u * 128:(u + 1) * 128]
            return acc + part
        acc = lax.fori_loop(0, n_chunks, body, jnp.zeros((rows, 128), I32))
        return jnp.sum(acc, axis=1, keepdims=True)

    t0 = jnp.where(count(lambda kc, c: kc >= 0) >= k, 0, INT_MIN).astype(I32)

    def bit_body(i, t):
        cand = t | jnp.left_shift(jnp.int32(1), 30 - i)
        return jnp.where(count(lambda kc, c: kc >= cand) >= k, cand, t)

    thr = lax.fori_loop(0, 31, bit_body, t0)
    n_gt = count(lambda kc, c: kc > thr)
    n_eq = count(lambda kc, c: kc == thr)
    need = k - n_gt
    tie = jnp.max(jnp.where((n_eq > need) & (thr > INT_MIN), 1, 0)) > 0

    def search():
        def idx_body(i, p):
            cand = p | jnp.left_shift(jnp.int32(1), idx_bits - 1 - i)

            def pred(kc, c):
                idx = c * cw + lax.broadcasted_iota(I32, kc.shape, 1)
                return (kc == thr) & (idx < cand)
            return jnp.where(count(pred) < need, cand, p)
        return lax.fori_loop(0, idx_bits, idx_body, jnp.zeros((rows, 1), I32))

    cut = lax.cond(tie, search, lambda: jnp.full((rows, 1), 2 ** 31 - 1, I32))
    return thr, cut


def _selected(kc, idx, thr, cut):
    return (kc > thr) | ((kc == thr) & (idx <= cut) & (thr > INT_MIN))


def _softmax_update(s, v, m_ref, l_ref, acc_ref, keep=None):
    m_old = m_ref[...]
    m_new = jnp.maximum(m_old, jnp.max(s, axis=1, keepdims=True))
    alpha = jnp.exp(m_old - m_new)
    p = jnp.exp(s - m_new)
    if keep is not None:
        p = jnp.where(keep, p, 0.0)
    l_ref[...] = alpha * l_ref[...] + jnp.sum(p, axis=1, keepdims=True)
    acc_ref[...] = alpha * acc_ref[...] + _dot(p.astype(BF16), v)
    m_ref[...] = m_new


def _ab_proj_kernel(x_ref, g_ref, w_ref, c128_ref, s128_ref, c64_ref, s64_ref,
                    qk_ref, v_ref, o_ref, qa_ref, ka_ref, kab_ref, va_ref, vab_ref,
                    qi_ref, ki_ref, kib_ref, gt_ref):
    h = _rms(x_ref[...], g_ref[...]).astype(BF16)

    def seg(a, b):
        return _dot(h, w_ref[:, a:b])

    qk_ref[...] = seg(0, 512)
    v_ref[...] = seg(512, 1024)
    o_ref[...] = seg(1024, 1536)
    c128, s128, c64, s64 = c128_ref[...], s128_ref[...], c64_ref[...], s64_ref[...]
    qa_ref[...] = _rope_wide(seg(1536, 2048), c128, s128, 128).astype(BF16)
    ka = _rope_wide(seg(2048, 2304), c128, s128, 128)
    ka_ref[...] = ka
    kab_ref[...] = ka.astype(BF16)
    va = seg(2304, 2560)
    va_ref[...] = va
    vab_ref[...] = va.astype(BF16)
    qi_ref[...] = _rope_wide(seg(2560, 2816), c64, s64, 64).astype(BF16)
    last = seg(2816, 2944)
    ki = _rope_piece(last, c64, s64, 64)[:, :64]
    ki_ref[...] = ki
    kib_ref[...] = ki.astype(BF16)
    gt_ref[...] = last[:, 64:]


def _ab_proj(x, g, w, tabs, tm):
    m = x.shape[0]
    c128, s128, c64, s64 = tabs
    tb = c128.shape[0] // tm
    row = lambda n: pl.BlockSpec((tm, n), lambda i: (i, 0))
    tab = pl.BlockSpec((tm, 128), lambda i: (i % tb, 0))
    full = lambda a: pl.BlockSpec(a.shape, lambda i: (0,) * a.ndim)
    widths = (512, 512, 512, 512, 256, 256, 256, 256, 256, 64, 64, 64)
    dtypes = (F32, F32, F32, BF16, F32, BF16, F32, BF16, BF16, F32, BF16, F32)
    return pl.pallas_call(
        _ab_proj_kernel,
        grid=(m // tm,),
        in_specs=[row(D_MODEL), full(g), full(w), tab, tab, tab, tab],
        out_specs=[row(n) for n in widths],
        out_shape=[jax.ShapeDtypeStruct((m, n), dt) for n, dt in zip(widths, dtypes)],
        compiler_params=_cp(("parallel",)),
    )(x, g, w, c128, s128, c64, s64)


def _conv_silu_kernel(x_ref, buf_ref, w_ref, sc_ref, o_ref, work, *, tm, hb, stride, taps):
    @pl.when(pl.program_id(1) == 0)
    def _():
        work[0:hb, :] = buf_ref[0]

    x = x_ref[...]
    work[hb:hb + tm, :] = x
    y = x * w_ref[taps - 1:taps, :]
    for j in range(taps - 1):
        y = y + work[pl.ds(hb - (taps - 1 - j) * stride, tm), :] * w_ref[j:j + 1, :]
    o_ref[...] = y * jax.nn.sigmoid(y) * sc_ref[...]
    work[0:hb, :] = x[tm - hb:, :]


def _conv_silu(x, bufp, w, scale, groups, tm, stride):
    m, c = x.shape
    hb = bufp.shape[1]
    nt = m // groups // tm
    taps = w.shape[0]
    return pl.pallas_call(
        functools.partial(_conv_silu_kernel, tm=tm, hb=hb, stride=stride, taps=taps),
        grid=(groups, nt),
        in_specs=[pl.BlockSpec((tm, c), lambda g, i: (g * nt + i, 0)),
                  pl.BlockSpec((1, hb, c), lambda g, i: (g, 0, 0)),
                  pl.BlockSpec(w.shape, lambda g, i: (0, 0)),
                  pl.BlockSpec(scale.shape, lambda g, i: (0, 0))],
        out_specs=pl.BlockSpec((tm, c), lambda g, i: (g * nt + i, 0)),
        out_shape=jax.ShapeDtypeStruct((m, c), F32),
        scratch_shapes=[pltpu.VMEM((hb + tm, c), F32)],
        compiler_params=_cp(("arbitrary", "arbitrary")),
    )(x, bufp, w, scale)


def _log_sigmoid(x):
    return jnp.minimum(x, 0.0) - jnp.log1p(jnp.exp(-jnp.abs(x)))


def _mlstm_kernel(qk_ref, v_ref, o_ref, gt_ref, gtt_ref, brow_ref, bcol_ref, gh_ref,
                  c0_ref, n0_ref, m0_ref, h_ref, c1_ref, n1_ref, m1_ref, c_s, n_s, m_s, *, cl, nc):
    c = pl.program_id(1)

    @pl.when(c == 0)
    def _():
        c_s[...] = c0_ref[0]
        for h in range(M_HEADS):
            n_s[h] = n0_ref[0, h:h + 1, :]
            m_s[h] = m0_ref[0, :, h:h + 1]

    qk = qk_ref[...]
    vv = v_ref[...]
    og = o_ref[...]
    gates = gt_ref[...] + brow_ref[...]
    gates_t = gtt_ref[0] + bcol_ref[...]
    li_col = gates[:, 0:4]
    lf_col = _log_sigmoid(gates[:, 4:8])
    li_row = gates_t[0:4, :]
    lf_row = _log_sigmoid(gates_t[4:8, :])
    r_i = lax.broadcasted_iota(I32, (cl, cl), 0)
    c_i = lax.broadcasted_iota(I32, (cl, cl), 1)
    causal = c_i <= r_i
    outs = []
    for h in range(M_HEADS):
        q = qk[:, h * M_QK:(h + 1) * M_QK]
        k = qk[:, 256 + h * M_QK:256 + (h + 1) * M_QK]
        v = vv[:, h * M_V:(h + 1) * M_V]
        cm = c_s[h]
        n = n_s[h]
        m_prev = m_s[h]
        b_col = jnp.sum(jnp.where(causal, lf_row[h:h + 1, :], 0.0), axis=1, keepdims=True)
        b_row = jnp.sum(jnp.where(r_i <= c_i, lf_col[:, h:h + 1], 0.0), axis=0, keepdims=True)
        dmat = jnp.where(causal, b_col - b_row + li_row[h:h + 1, :], -jnp.inf)
        inter = b_col + m_prev
        m_t = jnp.maximum(inter, jnp.max(dmat, axis=1, keepdims=True))
        iw = jnp.exp(inter - m_t)
        qb, kb, vb = q.astype(BF16), k.astype(BF16), v.astype(BF16)
        s = _dot_nt(qb, kb) * jnp.exp(dmat - m_t)
        num = iw * _dot(qb, cm.astype(BF16)) + _dot(s.astype(BF16), vb)
        den = iw * jnp.sum(q * n, axis=1, keepdims=True) + jnp.sum(s, axis=1, keepdims=True)
        hh = num / jnp.maximum(jnp.abs(den), jnp.exp(-m_t))
        m_new = m_t[cl - 1:cl, :]
        b_last = b_col[cl - 1:cl, :]
        w_end = jnp.exp(b_last - b_col + li_col[:, h:h + 1] - m_new)
        decay = jnp.exp(b_last + m_prev - m_new)
        kw = k * w_end
        c_s[h] = decay * cm + lax.dot_general(kw.astype(BF16), vb, (((0,), (0,)), ((), ())),
                                              preferred_element_type=F32)
        n_s[h] = decay * n + jnp.sum(kw, axis=0, keepdims=True)
        m_s[h] = m_new
        hn = _rms(hh, gh_ref[:, h * M_V:(h + 1) * M_V])
        outs.append(hn * jax.nn.sigmoid(og[:, h * M_V:(h + 1) * M_V]))
    h_ref[...] = jnp.concatenate(outs, axis=1)

    @pl.when(c == nc - 1)
    def _():
        c1_ref[0] = c_s[...]
        for h in range(M_HEADS):
            n1_ref[0, h:h + 1, :] = n_s[h]
            m1_ref[0, :, h:h + 1] = m_s[h]


def _mlstm(qkc, v, o, gates, gates_t, b_row, b_col, g_head, c0, n0, m0, batch, cl):
    m = qkc.shape[0]
    nc = m // batch // cl
    row = lambda n: pl.BlockSpec((cl, n), lambda b, c: (b * nc + c, 0))
    full = lambda a: pl.BlockSpec(a.shape, lambda b, c: (0,) * a.ndim)
    st_c = pl.BlockSpec((1, M_HEADS, M_QK, M_V), lambda b, c: (b, 0, 0, 0))
    st_n = pl.BlockSpec((1, M_HEADS, M_QK), lambda b, c: (b, 0, 0))
    st_m = pl.BlockSpec((1, 1, M_HEADS), lambda b, c: (b, 0, 0))
    return pl.pallas_call(
        functools.partial(_mlstm_kernel, cl=cl, nc=nc),
        grid=(batch, nc),
        in_specs=[row(512), row(512), row(512), row(64),
                  pl.BlockSpec((1, 8, cl), lambda b, c: (b, 0, c)),
                  full(b_row), full(b_col), full(g_head), st_c, st_n, st_m],
        out_specs=[row(512), st_c, st_n, st_m],
        out_shape=[jax.ShapeDtypeStruct((m, 512), F32),
                   jax.ShapeDtypeStruct((batch, M_HEADS, M_QK, M_V), F32),
                   jax.ShapeDtypeStruct((batch, M_HEADS, M_QK), F32),
                   jax.ShapeDtypeStruct((batch, 1, M_HEADS), F32)],
        scratch_shapes=[pltpu.VMEM((M_HEADS, M_QK, M_V), F32),
                        pltpu.VMEM((M_HEADS, 1, M_QK), F32),
                        pltpu.VMEM((M_HEADS, 1, 1), F32)],
        compiler_params=_cp(("arbitrary", "arbitrary")),
    )(qkc, v, o, gates, gates_t, b_row, b_col, g_head, c0, n0, m0)


def _dsa_prompt_kernel(qa_ref, qi_ref, gt_ref, kab_ref, vab_ref, kib_ref, o_ref,
                       keys, m_s, l_s, acc_s, *, tq, cw, topk, idx_bits):
    i = pl.program_id(1)
    n_ch = (i * tq + tq + cw - 1) // cw
    qpos = i * tq + lax.broadcasted_iota(I32, (tq, 1), 0)
    wi = gt_ref[:, 8:12] * IDX_SCALE
    qi = qi_ref[...]

    def score_body(c, carry):
        off = pl.multiple_of(c * cw, cw)
        kc = kib_ref[pl.ds(off, cw), :]
        sc = jnp.zeros((tq, cw), F32)
        for h in range(IDX_HEADS):
            rel = jnp.maximum(_dot_nt(qi[:, h * IDX_DIM:(h + 1) * IDX_DIM], kc), 0.0)
            sc = sc + rel * wi[:, h:h + 1]
        kpos = off + lax.broadcasted_iota(I32, (tq, cw), 1)
        keys[c] = jnp.where(kpos <= qpos, _float_key(sc), INT_MIN)
        return carry

    lax.fori_loop(0, n_ch, score_body, 0)
    thr, cut = _topk_threshold(keys, tq, n_ch, cw, topk, idx_bits)
    thr2 = jnp.concatenate([thr, thr], axis=0)
    cut2 = jnp.concatenate([cut, cut], axis=0)

    qa = qa_ref[...]
    qg = [jnp.concatenate([qa[:, (2 * g) * A_DIM:(2 * g + 1) * A_DIM],
                           qa[:, (2 * g + 1) * A_DIM:(2 * g + 2) * A_DIM]], axis=0) for g in range(A_KV)]
    m_s[...] = jnp.full(m_s.shape, NEG, F32)
    l_s[...] = jnp.zeros(l_s.shape, F32)
    acc_s[...] = jnp.zeros(acc_s.shape, F32)

    def att_body(c, carry):
        off = pl.multiple_of(c * cw, cw)
        kc = keys[c]
        kc2 = jnp.concatenate([kc, kc], axis=0)
        idx = off + lax.broadcasted_iota(I32, (2 * tq, cw), 1)
        sel = _selected(kc2, idx, thr2, cut2)
        for g in range(A_KV):
            kg = kab_ref[pl.ds(off, cw), g * A_DIM:(g + 1) * A_DIM]
            vg = vab_ref[pl.ds(off, cw), g * A_DIM:(g + 1) * A_DIM]
            s = jnp.where(sel, _dot_nt(qg[g], kg) * (A_DIM ** -0.5), NEG)
            _softmax_update(s, vg, m_s.at[g], l_s.at[g], acc_s.at[g], keep=sel)
        return carry

    lax.fori_loop(0, n_ch, att_body, 0)
    outs = []
    for g in range(A_KV):
        og = acc_s[g] / l_s[g]
        outs += [og[0:tq], og[tq:2 * tq]]
    o_ref[...] = jnp.concatenate(outs, axis=1)


def _dsa_prompt(qa, qi, gates, kab, vab, kib, batch, topk, tq):
    m = qa.shape[0]
    t = m // batch
    nq = t // tq
    cw = min(512, t)
    row = lambda n: pl.BlockSpec((tq, n), lambda b, i: (b * nq + i, 0))
    whole = lambda n: pl.BlockSpec((t, n), lambda b, i: (b, 0))
    return pl.pallas_call(
        functools.partial(_dsa_prompt_kernel, tq=tq, cw=cw, topk=topk,
                          idx_bits=max(1, (t - 1).bit_length())),
        grid=(batch, nq),
        in_specs=[row(512), row(256), row(64), whole(256), whole(256), whole(64)],
        out_specs=row(512),
        out_shape=jax.ShapeDtypeStruct((m, 512), F32),
        scratch_shapes=[pltpu.VMEM((t // cw, tq, cw), I32),
                        pltpu.VMEM((A_KV, 2 * tq, 1), F32),
                        pltpu.VMEM((A_KV, 2 * tq, 1), F32),
                        pltpu.VMEM((A_KV, 2 * tq, A_DIM), F32)],
        compiler_params=_cp(("parallel", "arbitrary")),
    )(qa, qi, gates, kab, vab, kib)


def _idx_scores(qi, gates, keys_mat, keys_on_lanes):
    qs = jnp.concatenate([qi[:, h * IDX_DIM:(h + 1) * IDX_DIM] for h in range(IDX_HEADS)], axis=0)
    ws = jnp.concatenate([gates[:, 8 + h:9 + h] for h in range(IDX_HEADS)], axis=0) * IDX_SCALE
    qk = _dot(qs, keys_mat) if keys_on_lanes else _dot_nt(qs, keys_mat)
    rel = jnp.maximum(qk, 0.0) * ws
    t = qi.shape[0]
    sc = rel[0:t]
    for h in range(1, IDX_HEADS):
        sc = sc + rel[h * t:(h + 1) * t]
    return sc


def _dsa_scores_sample_kernel(pt_ref, qi_ref, gt_ref, knew_ref, *rest, pp):
    pages = rest[:pp]
    kp_ref, kn_ref = rest[pp], rest[pp + 1]
    qi = qi_ref[...]
    gates = gt_ref[...]
    kcat = jnp.concatenate([p[...] for p in pages], axis=1).astype(BF16)
    kp_ref[0] = _float_key(_idx_scores(qi, gates, kcat, True))
    sn = _idx_scores(qi, gates, knew_ref[0], False)
    t = qi.shape[0]
    vis = lax.broadcasted_iota(I32, (t, PAGE), 1) <= lax.broadcasted_iota(I32, (t, PAGE), 0)
    kn_ref[0] = jnp.where(vis, _float_key(sn), INT_MIN)


def _page_specs(shape_tail, layer, n_pages, pp):
    nd = len(shape_tail)

    def spec(u):
        return pl.BlockSpec((None, None) + shape_tail,
                            lambda b, j, pt: (layer, pt[b * n_pages + j * pp + u]) + (0,) * nd)
    return [spec(u) for u in range(pp)]


def _dsa_scores_sample(pt, qi, gates, knew, pool, layer, batch, tt, n_pages, pp):
    grid_spec = pltpu.PrefetchScalarGridSpec(
        num_scalar_prefetch=1,
        grid=(batch, n_pages // pp),
        in_specs=[pl.BlockSpec((tt, 256), lambda b, j, pt: (b, 0)),
                  pl.BlockSpec((tt, 64), lambda b, j, pt: (b, 0)),
                  pl.BlockSpec((1, PAGE, IDX_DIM), lambda b, j, pt: (b, 0, 0))]
                 + _page_specs((IDX_DIM, PAGE), layer, n_pages, pp),
        out_specs=[pl.BlockSpec((1, tt, pp * PAGE), lambda b, j, pt: (b, 0, j)),
                   pl.BlockSpec((1, tt, PAGE), lambda b, j, pt: (b, 0, 0))],
    )
    return pl.pallas_call(
        functools.partial(_dsa_scores_sample_kernel, pp=pp),
        grid_spec=grid_spec,
        out_shape=[jax.ShapeDtypeStruct((batch, tt, n_pages * PAGE), I32),
                   jax.ShapeDtypeStruct((batch, tt, PAGE), I32)],
        compiler_params=_cp(("parallel", "arbitrary")),
    )(pt, qi, gates, knew, *([pool] * pp))


def _thresh_sample_kernel(kp_ref, kn_ref, thr_ref, cut_ref, keys, *, rows, cw, n_past, topk, idx_bits):
    for c in range(n_past):
        keys[c] = kp_ref[:, :, c * cw:(c + 1) * cw].reshape(rows, cw)
    keys[n_past] = jnp.concatenate([kn_ref[...].reshape(rows, PAGE),
                                    jnp.full((rows, cw - PAGE), INT_MIN, I32)], axis=1)
    thr, cut = _topk_threshold(keys, rows, n_past + 1, cw, topk, idx_bits)
    thr_ref[...] = jnp.broadcast_to(thr, (rows, PAGE)).reshape(thr_ref.shape)
    cut_ref[...] = jnp.broadcast_to(cut, (rows, PAGE)).reshape(cut_ref.shape)


def _thresh_sample(keys_past, keys_new, topk, bg):
    batch, tt, lp = keys_past.shape
    cw = 512
    rows = bg * tt
    n_past = lp // cw
    blk = lambda n: pl.BlockSpec((bg, tt, n), lambda i: (i, 0, 0))
    return pl.pallas_call(
        functools.partial(_thresh_sample_kernel, rows=rows, cw=cw, n_past=n_past, topk=topk,
                          idx_bits=(lp + cw - 1).bit_length()),
        grid=(batch // bg,),
        in_specs=[blk(lp), blk(PAGE)],
        out_specs=[blk(PAGE), blk(PAGE)],
        out_shape=[jax.ShapeDtypeStruct((batch, tt, PAGE), I32)] * 2,
        scratch_shapes=[pltpu.VMEM((n_past + 1, rows, cw), I32)],
        compiler_params=_cp(("parallel",)),
    )(keys_past, keys_new)


def _dsa_attend_sample_kernel(pt_ref, qa_ref, thr_ref, cut_ref, kp_ref, kn_ref, knew_ref, vnew_ref, *rest,
                              pp, n_steps, n_past_keys):
    kpages, vpages = rest[:pp], rest[pp:2 * pp]
    o_ref = rest[2 * pp]
    m_s, l_s, acc_s = rest[2 * pp + 1:]
    j = pl.program_id(1)
    tt = qa_ref.shape[0]

    @pl.when(j == 0)
    def _():
        m_s[...] = jnp.full(m_s.shape, NEG, F32)
        l_s[...] = jnp.zeros(l_s.shape, F32)
        acc_s[...] = jnp.zeros(acc_s.shape, F32)

    qa = qa_ref[...]
    qg = [jnp.concatenate([qa[:, (2 * g) * A_DIM:(2 * g + 1) * A_DIM],
                           qa[:, (2 * g + 1) * A_DIM:(2 * g + 2) * A_DIM]], axis=0) for g in range(A_KV)]
    thr = thr_ref[0][:, 0:1]
    cut = cut_ref[0][:, 0:1]
    thr2 = jnp.concatenate([thr, thr], axis=0)
    cut2 = jnp.concatenate([cut, cut], axis=0)

    def attend(kc, first_idx, kv_of_group):
        nk = kc.shape[1]
        kc2 = jnp.concatenate([kc, kc], axis=0)
        idx = first_idx + lax.broadcasted_iota(I32, (2 * tt, nk), 1)
        sel = _selected(kc2, idx, thr2, cut2)
        for g in range(A_KV):
            kg, vg = kv_of_group(g)
            s = jnp.where(sel, _dot_nt(qg[g], kg) * (A_DIM ** -0.5), NEG)
            _softmax_update(s, vg, m_s.at[g], l_s.at[g], acc_s.at[g], keep=sel)

    def paged(g):
        rows = lambda p: p[pl.ds(g, PAGE, stride=A_KV), :]
        return (jnp.concatenate([rows(p) for p in kpages], axis=0).astype(BF16),
                jnp.concatenate([rows(p) for p in vpages], axis=0).astype(BF16))

    attend(kp_ref[0], j * (pp * PAGE), paged)

    @pl.when(j == n_steps - 1)
    def _():
        attend(kn_ref[0], n_past_keys,
               lambda g: (knew_ref[0][:, g * A_DIM:(g + 1) * A_DIM], vnew_ref[0][:, g * A_DIM:(g + 1) * A_DIM]))
        outs = []
        for g in range(A_KV):
            og = acc_s[g] / l_s[g]
            outs += [og[0:tt], og[tt:2 * tt]]
        o_ref[...] = jnp.concatenate(outs, axis=1)


def _dsa_attend_sample(pt, qa, thr, cut, keys_past, keys_new, knew, vnew, pool_k, pool_v,
                       layer, batch, tt, n_pages, pp):
    n_steps = n_pages // pp
    kvw = A_KV * A_DIM
    grid_spec = pltpu.PrefetchScalarGridSpec(
        num_scalar_prefetch=1,
        grid=(batch, n_steps),
        in_specs=[pl.BlockSpec((tt, 512), lambda b, j, pt: (b, 0)),
                  pl.BlockSpec((1, tt, PAGE), lambda b, j, pt: (b, 0, 0)),
                  pl.BlockSpec((1, tt, PAGE), lambda b, j, pt: (b, 0, 0)),
                  pl.BlockSpec((1, tt, pp * PAGE), lambda b, j, pt: (b, 0, j)),
                  pl.BlockSpec((1, tt, PAGE), lambda b, j, pt: (b, 0, 0)),
                  pl.BlockSpec((1, PAGE, kvw), lambda b, j, pt: (b, 0, 0)),
                  pl.BlockSpec((1, PAGE, kvw), lambda b, j, pt: (b, 0, 0))]
                 + _page_specs((PAGE * A_KV, A_DIM), layer, n_pages, pp)
                 + _page_specs((PAGE * A_KV, A_DIM), layer, n_pages, pp),
        out_specs=pl.BlockSpec((tt, 512), lambda b, j, pt: (b, 0)),
        scratch_shapes=[pltpu.VMEM((A_KV, 2 * tt, 1), F32),
                        pltpu.VMEM((A_KV, 2 * tt, 1), F32),
                        pltpu.VMEM((A_KV, 2 * tt, A_DIM), F32)],
    )
    return pl.pallas_call(
        functools.partial(_dsa_attend_sample_kernel, pp=pp, n_steps=n_steps, n_past_keys=n_pages * PAGE),
        grid_spec=grid_spec,
        out_shape=jax.ShapeDtypeStruct((batch * tt, 512), F32),
        compiler_params=_cp(("parallel", "arbitrary")),
    )(pt, qa, thr, cut, keys_past, keys_new, knew, vnew, *([pool_k] * pp), *([pool_v] * pp))


def _mla_pre_kernel(x_ref, g_ref, win_ref, gq_ref, gkv_ref, wuq_ref, wuk_ref, c64_ref, s64_ref,
                    ckv_ref, ckvb_ref, kr_ref, krb_ref, ql_ref, qr_ref):
    h = _rms(x_ref[...], g_ref[...]).astype(BF16)
    cq = _rms(_dot(h, win_ref[:, 0:Q_LORA]), gq_ref[...])
    rest = _dot(h, win_ref[:, Q_LORA:Q_LORA + 256])
    ckv = _rms(rest[:, 0:KV_LORA], gkv_ref[...])
    ckv_ref[...] = ckv
    ckvb_ref[...] = ckv.astype(BF16)
    c64, s64 = c64_ref[...], s64_ref[...]
    kr = _rope_piece(rest[:, 128:256], c64, s64, 64)[:, :64]
    kr_ref[...] = kr
    krb_ref[...] = kr.astype(BF16)
    q = _dot(cq.astype(BF16), wuq_ref[...])
    qr_ref[...] = _rope_wide(q[:, 1024:1536], c64, s64, 64).astype(BF16)
    ql_ref[...] = jnp.concatenate(
        [_dot(q[:, hh * NOPE:(hh + 1) * NOPE].astype(BF16), wuk_ref[hh]) for hh in range(C_HEADS)],
        axis=1).astype(BF16)


def _mla_pre(x, g, win, gq, gkv, wuq, wuk, tabs, tm):
    m = x.shape[0]
    c64, s64 = tabs[2], tabs[3]
    tb = c64.shape[0] // tm
    row = lambda n: pl.BlockSpec((tm, n), lambda i: (i, 0))
    tab = pl.BlockSpec((tm, 128), lambda i: (i % tb, 0))
    full = lambda a: pl.BlockSpec(a.shape, lambda i: (0,) * a.ndim)
    widths = (128, 128, 64, 64, 1024, 512)
    dtypes = (F32, BF16, F32, BF16, BF16, BF16)
    return pl.pallas_call(
        _mla_pre_kernel,
        grid=(m // tm,),
        in_specs=[row(D_MODEL), full(g), full(win), full(gq), full(gkv), full(wuq), full(wuk), tab, tab],
        out_specs=[row(n) for n in widths],
        out_shape=[jax.ShapeDtypeStruct((m, n), dt) for n, dt in zip(widths, dtypes)],
        compiler_params=_cp(("parallel",)),
    )(x, g, win, gq, gkv, wuq, wuk, c64, s64)


def _stack_heads(ql, qr):
    qls = jnp.concatenate([ql[:, h * KV_LORA:(h + 1) * KV_LORA] for h in range(C_HEADS)], axis=0)
    qrs = jnp.concatenate([qr[:, h * ROPE_D:(h + 1) * ROPE_D] for h in range(C_HEADS)], axis=0)
    return qls, qrs


def _mla_prompt_kernel(ql_ref, qr_ref, ckv_ref, kr_ref, o_ref, m_s, l_s, acc_s, *, tq):
    i = pl.program_id(1)
    qls, qrs = _stack_heads(ql_ref[...], qr_ref[...])
    m_s[...] = jnp.full(m_s.shape, NEG, F32)
    l_s[...] = jnp.zeros(l_s.shape, F32)
    acc_s[...] = jnp.zeros(acc_s.shape, F32)

    def scores(c):
        off = pl.multiple_of(c * tq, tq)
        kc = ckv_ref[pl.ds(off, tq), :]
        rc = kr_ref[pl.ds(off, tq), :]
        return (_dot_nt(qls, kc) + _dot_nt(qrs, rc)) * MLA_SCALE, kc

    def body(c, carry):
        s, kc = scores(c)
        _softmax_update(s, kc, m_s, l_s, acc_s)
        return carry

    lax.fori_loop(0, i, body, 0)
    s, kc = scores(i)
    rows = C_HEADS * tq
    t_in = lax.broadcasted_iota(I32, (rows, tq), 0) & (tq - 1)
    s = jnp.where(lax.broadcasted_iota(I32, (rows, tq), 1) <= t_in, s, NEG)
    _softmax_update(s, kc, m_s, l_s, acc_s)
    o = acc_s[...] / l_s[...]
    o_ref[...] = jnp.concatenate([o[h * tq:(h + 1) * tq] for h in range(C_HEADS)], axis=1).astype(BF16)


def _mla_prompt(ql, qr, ckvb, krb, batch, tq):
    m = ql.shape[0]
    t = m // batch
    nq = t // tq
    row = lambda n: pl.BlockSpec((tq, n), lambda b, i: (b * nq + i, 0))
    whole = lambda n: pl.BlockSpec((t, n), lambda b, i: (b, 0))
    return pl.pallas_call(
        functools.partial(_mla_prompt_kernel, tq=tq),
        grid=(batch, nq),
        in_specs=[row(1024), row(512), whole(KV_LORA), whole(ROPE_D)],
        out_specs=row(1024),
        out_shape=jax.ShapeDtypeStruct((m, 1024), BF16),
        scratch_shapes=[pltpu.VMEM((C_HEADS * tq, 1), F32),
                        pltpu.VMEM((C_HEADS * tq, 1), F32),
                        pltpu.VMEM((C_HEADS * tq, KV_LORA), F32)],
        compiler_params=_cp(("parallel", "arbitrary")),
    )(ql, qr, ckvb, krb)


def _mla_sample_kernel(pt_ref, ql_ref, qr_ref, cnew_ref, rnew_ref, *rest, pp, n_steps):
    cpages, rpages = rest[:pp], rest[pp:2 * pp]
    o_ref = rest[2 * pp]
    m_s, l_s, acc_s = rest[2 * pp + 1:]
    j = pl.program_id(1)
    tt = ql_ref.shape[0]

    @pl.when(j == 0)
    def _():
        m_s[...] = jnp.full(m_s.shape, NEG, F32)
        l_s[...] = jnp.zeros(l_s.shape, F32)
        acc_s[...] = jnp.zeros(acc_s.shape, F32)

    qls, qrs = _stack_heads(ql_ref[...], qr_ref[...])
    ccat = jnp.concatenate([p[...] for p in cpages], axis=0).astype(BF16)
    rcat = jnp.concatenate([p[...] for p in rpages], axis=1).astype(BF16)
    s = (_dot_nt(qls, ccat) + _dot(qrs, rcat)) * MLA_SCALE
    _softmax_update(s, ccat, m_s, l_s, acc_s)

    @pl.when(j == n_steps - 1)
    def _():
        cn = cnew_ref[0]
        rows = C_HEADS * tt
        sn = (_dot_nt(qls, cn) + _dot_nt(qrs, rnew_ref[0])) * MLA_SCALE
        t_in = lax.broadcasted_iota(I32, (rows, PAGE), 0) & (tt - 1)
        sn = jnp.where(lax.broadcasted_iota(I32, (rows, PAGE), 1) <= t_in, sn, NEG)
        _softmax_update(sn, cn, m_s, l_s, acc_s)
        o = acc_s[...] / l_s[...]
        o_ref[...] = jnp.concatenate([o[h * tt:(h + 1) * tt] for h in range(C_HEADS)], axis=1).astype(BF16)


def _mla_sample(pt, ql, qr, cnew, rnew, pool_c, pool_r, layer, batch, tt, n_pages, pp):
    n_steps = n_pages // pp
    grid_spec = pltpu.PrefetchScalarGridSpec(
        num_scalar_prefetch=1,
        grid=(batch, n_steps),
        in_specs=[pl.BlockSpec((tt, 1024), lambda b, j, pt: (b, 0)),
                  pl.BlockSpec((tt, 512), lambda b, j, pt: (b, 0)),
                  pl.BlockSpec((1, PAGE, KV_LORA), lambda b, j, pt: (b, 0, 0)),
                  pl.BlockSpec((1, PAGE, ROPE_D), lambda b, j, pt: (b, 0, 0))]
                 + _page_specs((PAGE, KV_LORA), layer, n_pages, pp)
                 + _page_specs((ROPE_D, PAGE), layer, n_pages, pp),
        out_specs=pl.BlockSpec((tt, 1024), lambda b, j, pt: (b, 0)),
        scratch_shapes=[pltpu.VMEM((C_HEADS * tt, 1), F32),
                        pltpu.VMEM((C_HEADS * tt, 1), F32),
                        pltpu.VMEM((C_HEADS * tt, KV_LORA), F32)],
    )
    return pl.pallas_call(
        functools.partial(_mla_sample_kernel, pp=pp, n_steps=n_steps),
        grid_spec=grid_spec,
        out_shape=jax.ShapeDtypeStruct((batch * tt, 1024), BF16),
        compiler_params=_cp(("parallel", "arbitrary")),
    )(pt, ql, qr, cnew, rnew, *([pool_c] * pp), *([pool_r] * pp))


def _ab_out_kernel(x_ref, hm_ref, oa_ref, w_ref, o_ref):
    o_ref[...] = (x_ref[...] + _dot(hm_ref[...].astype(BF16), w_ref[0:512, :])
                  + _dot(oa_ref[...].astype(BF16), w_ref[512:1024, :]))


def _ab_out(x, hm, oa, w, tm):
    m = x.shape[0]
    row = lambda n: pl.BlockSpec((tm, n), lambda i: (i, 0))
    return pl.pallas_call(
        _ab_out_kernel,
        grid=(m // tm,),
        in_specs=[row(D_MODEL), row(512), row(512), pl.BlockSpec(w.shape, lambda i: (0, 0))],
        out_specs=row(D_MODEL),
        out_shape=jax.ShapeDtypeStruct((m, D_MODEL), F32),
        compiler_params=_cp(("parallel",)),
    )(x, hm, oa, w)


def _mla_out_kernel(x_ref, ol_ref, wuv_ref, w_ref, o_ref):
    ol = ol_ref[...]
    o = jnp.concatenate([_dot(ol[:, h * KV_LORA:(h + 1) * KV_LORA], wuv_ref[h]) for h in range(C_HEADS)],
                        axis=1)
    o_ref[...] = x_ref[...] + _dot(o.astype(BF16), w_ref[...])


def _mla_out(x, ol, wuv, w, tm):
    m = x.shape[0]
    row = lambda n: pl.BlockSpec((tm, n), lambda i: (i, 0))
    return pl.pallas_call(
        _mla_out_kernel,
        grid=(m // tm,),
        in_specs=[row(D_MODEL), row(1024), pl.BlockSpec(wuv.shape, lambda i: (0, 0, 0)),
                  pl.BlockSpec(w.shape, lambda i: (0, 0))],
        out_specs=row(D_MODEL),
        out_shape=jax.ShapeDtypeStruct((m, D_MODEL), F32),
        compiler_params=_cp(("parallel",)),
    )(x, ol, wuv, w)


def _ffn_kernel(x_ref, g_ref, wa_ref, wg_ref, wc_ref, bc_ref, wd_ref, buf_ref, gf_ref,
                o_ref, st_ref, y_ref, h_s, acc_s, halo, work, *, tm, hb, stride, nj, final):
    i = pl.program_id(1)
    j = pl.program_id(2)

    @pl.when(j == 0)
    def _():
        h_s[...] = _rms(x_ref[...], g_ref[...]).astype(BF16)
        acc_s[...] = jnp.zeros(acc_s.shape, F32)

    @pl.when(i == 0)
    def _():
        halo[j] = buf_ref[0]

    h = h_s[...]
    a = _dot(h, wa_ref[...])
    gg = _dot(h, wg_ref[...])
    work[0:hb, :] = halo[j]
    work[hb:hb + tm, :] = gg
    gc = gg * wc_ref[FFN_CONV - 1:FFN_CONV, :]
    for t in range(FFN_CONV - 1):
        gc = gc + work[pl.ds(hb - (FFN_CONV - 1 - t) * stride, tm), :] * wc_ref[t:t + 1, :]
    last = gg[tm - hb:, :]
    halo[j] = last
    st_ref[0, 0] = last
    gc = gc + bc_ref[...]
    p = a * (gc * jax.nn.sigmoid(gc))
    acc_s[...] += _dot(p.astype(BF16), wd_ref[...])

    @pl.when(j == nj - 1)
    def _():
        xn = x_ref[...] + acc_s[...]
        o_ref[...] = xn
        if final:
            y_ref[...] = _rms(xn, gf_ref[...])


def _ffn(x, g, wa, wg, wc, bc, wd, bufp, g_final, groups, tm, fc, stride, final):
    m = x.shape[0]
    hb = bufp.shape[1]
    nt = m // groups // tm
    nj = D_FF // fc
    row = pl.BlockSpec((tm, D_MODEL), lambda gi, i, j: (gi * nt + i, 0))
    vec = lambda a: pl.BlockSpec(a.shape, lambda gi, i, j: (0, 0))
    st = pl.BlockSpec((1, hb, fc), lambda gi, i, j: (gi, 0, j))
    st_out = pl.BlockSpec((1, 1, hb, fc), lambda gi, i, j: (gi, i, 0, j))
    outs = [row, st_out] + ([row] if final else [])
    shapes = [jax.ShapeDtypeStruct((m, D_MODEL), F32), jax.ShapeDtypeStruct((groups, nt, hb, D_FF), F32)]
    if final:
        shapes.append(jax.ShapeDtypeStruct((m, D_MODEL), F32))

    def kern(*refs):
        if final:
            return _ffn_kernel(*refs, tm=tm, hb=hb, stride=stride, nj=nj, final=True)
        ins, rest = refs[:9], refs[9:]
        return _ffn_kernel(*ins, rest[0], rest[1], None, *rest[2:], tm=tm, hb=hb, stride=stride, nj=nj,
                           final=False)

    return pl.pallas_call(
        kern,
        grid=(groups, nt, nj),
        in_specs=[row, vec(g),
                  pl.BlockSpec((D_MODEL, fc), lambda gi, i, j: (0, j)),
                  pl.BlockSpec((D_MODEL, fc), lambda gi, i, j: (0, j)),
                  pl.BlockSpec((FFN_CONV, fc), lambda gi, i, j: (0, j)),
                  pl.BlockSpec((1, fc), lambda gi, i, j: (0, j)),
                  pl.BlockSpec((fc, D_MODEL), lambda gi, i, j: (j, 0)),
                  st, vec(g_final)],
        out_specs=outs,
        out_shape=shapes,
        scratch_shapes=[pltpu.VMEM((tm, D_MODEL), BF16),
                        pltpu.VMEM((tm, D_MODEL), F32),
                        pltpu.VMEM((nj, hb, fc), F32),
                        pltpu.VMEM((hb + tm, fc), F32)],
        compiler_params=_cp(("arbitrary", "arbitrary", "arbitrary")),
    )(x, g, wa, wg, wc, bc, wd, bufp, g_final)


def _rope_tables(pos, d):
    half = d // 2
    inv = ROPE_THETA ** (-jnp.arange(half, dtype=F32) * (2.0 / d))
    ang = pos.astype(F32)[:, None] * inv[None, :]
    cos, sin = jnp.cos(ang), jnp.sin(ang)
    reps = 128 // d
    return (jnp.tile(jnp.concatenate([cos, cos], axis=1), (1, reps)),
            jnp.tile(jnp.concatenate([-sin, sin], axis=1), (1, reps)))


def _prep_weights(w_in_ab, w_out_ab, w_in_mla, w_uq, w_uk, w_uv, w_out_mla, w_up, w_down):
    cuts = np.cumsum((0,) + AB_WIDTHS)
    seg = lambda w, k: w[:, :, cuts[k]:cuts[k + 1]]
    n_ab = w_in_ab.shape[0]
    pad = jnp.zeros((n_ab, D_MODEL, 52), F32)
    w_ab = jnp.concatenate([seg(w_in_ab, 0), seg(w_in_ab, 1), seg(w_in_ab, 2), seg(w_in_ab, 5),
                            seg(w_in_ab, 6), seg(w_in_ab, 7), seg(w_in_ab, 8), seg(w_in_ab, 9),
                            seg(w_in_ab, 3), seg(w_in_ab, 4), seg(w_in_ab, 10), pad], axis=2).astype(BF16)
    n_c = w_in_mla.shape[0]
    w_mla = jnp.concatenate([w_in_mla, jnp.zeros((n_c, D_MODEL, 64), F32)], axis=2).astype(BF16)
    uq = w_uq.reshape(n_c, Q_LORA, C_HEADS, NOPE + ROPE_D)
    w_uq2 = jnp.concatenate([uq[..., :NOPE].reshape(n_c, Q_LORA, C_HEADS * NOPE),
                             uq[..., NOPE:].reshape(n_c, Q_LORA, C_HEADS * ROPE_D)], axis=2).astype(BF16)
    return dict(w_ab=w_ab, w_out_ab=w_out_ab.astype(BF16), w_mla=w_mla, w_uq=w_uq2,
                w_uk=w_uk.astype(BF16), w_uv=w_uv.astype(BF16), w_out_mla=w_out_mla.astype(BF16),
                w_a=w_up[:, :, :D_FF].astype(BF16), w_g=w_up[:, :, D_FF:].astype(BF16),
                w_down=w_down.astype(BF16))


def _front_pad(buf, hb):
    g, r, c = buf.shape
    return jnp.concatenate([jnp.zeros((g, hb - r, c), F32), buf], axis=1)


def _trunk(x, pos_rows, P, W, past, cfg):
    batch, tt = cfg["batch"], cfg["t"]
    tm, tq_dsa, tq_mla = cfg["tm"], cfg["tq_dsa"], cfg["tq_mla"]
    m = batch * tt
    c128, s128 = _rope_tables(pos_rows, 128)
    c64, s64 = _rope_tables(pos_rows, 64)
    tabs = (c128, s128, c64, s64)
    kscale = jnp.concatenate([jnp.ones((1, 256), F32), jnp.full((1, 256), M_QK ** -0.5, F32)], axis=1)
    ab_states, c_states, ffn_states = [], [], []
    y_final = None
    for l in range(DEPTH):
        j = l // 2
        g_attn = P["g_attn"][l][None, :]
        if l % 2 == 0:
            (qk_m, v_m, o_m, qa, ka, kab, va, vab, qi, ki, kib, gates) = _ab_proj(x, g_attn, W["w_ab"][j], tabs, tm)
            if past is None:
                conv_buf = jnp.zeros((batch, M_CONV - 1, 512), F32)
                c0 = jnp.zeros((batch, M_HEADS, M_QK, M_V), F32)
                n0 = jnp.zeros((batch, M_HEADS, M_QK), F32)
                m0 = jnp.zeros((batch, 1, M_HEADS), F32)
            else:
                conv_buf = past["mconv"][j]
                c0, n0, m0 = past["C"][j], past["n"][j], past["m"][j][:, None, :]
            qkc = _conv_silu(qk_m, _front_pad(conv_buf, 8), P["w_mconv"][j], kscale,
                             groups=batch, tm=min(tm, tt), stride=1)
            cl = math.gcd(tt, M_CHUNK)
            gates_t = jnp.transpose(gates[:, :8].reshape(batch, tt, 8), (0, 2, 1))
            bias = jnp.concatenate([P["b_igate"][j], P["b_fgate"][j]])
            b_row = jnp.concatenate([bias, jnp.zeros((56,), F32)])[None, :]
            h_m, c1, n1, m1 = _mlstm(qkc, v_m, o_m, gates, gates_t, b_row, bias[:, None],
                                     P["g_mhead"][j].reshape(1, 512), c0, n0, m0, batch, cl)
            if past is None:
                o_a = _dsa_prompt(qa, qi, gates, kab, vab, kib, batch, min(DSA_TOPK, tt // 4), tq_dsa)
            else:
                pt, n_pages, pp = past["pt"], past["n_pages"], cfg["pp"]
                pad_page = lambda a: jnp.pad(a.reshape(batch, tt, -1), ((0, 0), (0, PAGE - tt), (0, 0)))
                keys_past, keys_new = _dsa_scores_sample(pt, qi, gates, pad_page(kib), past["kidx"], j,
                                                         batch, tt, n_pages, pp)
                topk = min(DSA_TOPK, (n_pages * PAGE + tt) // 4)
                thr, cut = _thresh_sample(keys_past, keys_new, topk, cfg["bg"])
                o_a = _dsa_attend_sample(pt, qa, thr, cut, keys_past, keys_new, pad_page(kab), pad_page(vab),
                                         past["k"], past["v"], j, batch, tt, n_pages, pp)
            x = _ab_out(x, h_m, o_a, W["w_out_ab"][j], tm)
            ab_states.append((c1, n1, m1.reshape(batch, M_HEADS),
                              qk_m.reshape(batch, tt, 512)[:, tt - (M_CONV - 1):],
                              ka.reshape(batch, tt, A_KV, A_DIM), va.reshape(batch, tt, A_KV, A_DIM),
                              ki.reshape(batch, tt, IDX_DIM)))
        else:
            ckv, ckvb, kr, krb, ql, qr = _mla_pre(x, g_attn, W["w_mla"][j], P["g_cq"][j][None, :],
                                                  P["g_ckv"][j][None, :], W["w_uq"][j], W["w_uk"][j], tabs, tm)
            if past is None:
                ol = _mla_prompt(ql, qr, ckvb, krb, batch, tq_mla)
            else:
                pad_page = lambda a: jnp.pad(a.reshape(batch, tt, -1), ((0, 0), (0, PAGE - tt), (0, 0)))
                ol = _mla_sample(past["pt"], ql, qr, pad_page(ckvb), pad_page(krb), past["ckv"], past["kr"], j,
                                 batch, tt, past["n_pages"], cfg["pp"])
            x = _mla_out(x, ol, W["w_uv"][j], W["w_out_mla"][j], tm)
            c_states.append((ckv.reshape(batch, tt, KV_LORA), kr.reshape(batch, tt, ROPE_D)))
        final = l == DEPTH - 1
        g_ffn = P["g_ffn"][l][None, :]
        if past is None:
            xin, groups, stride = x, batch, 1
            bufp = jnp.zeros((batch, 8, D_FF), F32)
        else:
            xin = jnp.transpose(x.reshape(batch, tt, D_MODEL), (1, 0, 2)).reshape(m, D_MODEL)
            groups, stride = 1, batch
            bufp = jnp.transpose(past["fconv"][l], (1, 0, 2)).reshape(1, (FFN_CONV - 1) * batch, D_FF)
        res = _ffn(xin, g_ffn, W["w_a"][l], W["w_g"][l], P["w_fconv"][l], P["b_fconv"][l][None, :],
                   W["w_down"][l], bufp, P["g_final"][None, :], groups, cfg["tm_ffn"], cfg["fc"], stride, final)
        xo, st = res[0], res[1][:, -1]
        if past is None:
            x = xo
            ffn_states.append(st[:, 8 - (FFN_CONV - 1):, :])
            if final:
                y_final = res[2].reshape(batch, tt, D_MODEL)
        else:
            unt = lambda a: jnp.transpose(a.reshape(tt, batch, D_MODEL), (1, 0, 2))
            x = unt(xo).reshape(m, D_MODEL)
            ffn_states.append(jnp.transpose(st.reshape(FFN_CONV - 1, batch, D_FF), (1, 0, 2)))
            if final:
                y_final = unt(res[2])
    ab = [jnp.stack(s) for s in zip(*ab_states)]
    cc = [jnp.stack(s) for s in zip(*c_states)]
    return (y_final, *ab, *cc, jnp.stack(ffn_states))


def kernel(x_prompt, x_sample, state_mlstm_C, state_mlstm_n, state_mlstm_m, state_mlstm_conv, cache_dsa_k, cache_dsa_v, cache_dsa_kidx, cache_mla_ckv, cache_mla_krope, state_ffn_conv, page_table, g_attn, g_ffn, g_final, w_in_ab, w_mconv, b_igate, b_fgate, g_mhead, w_out_ab, w_in_mla, g_cq, g_ckv, w_uq, w_uk, w_uv, w_out_mla, w_up, w_fconv, b_fconv, w_down):
    W = _prep_weights(w_in_ab, w_out_ab, w_in_mla, w_uq, w_uk, w_uv, w_out_mla, w_up, w_down)
    P = dict(g_attn=g_attn, g_ffn=g_ffn, g_final=g_final, w_mconv=w_mconv, b_igate=b_igate, b_fgate=b_fgate,
             g_mhead=g_mhead, g_cq=g_cq, g_ckv=g_ckv, w_fconv=w_fconv, b_fconv=b_fconv)
    bp, tp, _ = x_prompt.shape
    bs, ts, _ = x_sample.shape
    n_pages = page_table.shape[1]
    n_pool = cache_dsa_k.shape[1]

    cfg_p = dict(batch=bp, t=tp, tm=min(512, tp), tq_dsa=min(128, tp), tq_mla=min(256, tp),
                 tm_ffn=min(512, tp), fc=1408)
    out_p = _trunk(x_prompt.reshape(bp * tp, D_MODEL), jnp.arange(tp), P, W, None, cfg_p)

    past = dict(C=state_mlstm_C, n=state_mlstm_n, m=state_mlstm_m, mconv=state_mlstm_conv,
                k=cache_dsa_k.reshape(cache_dsa_k.shape[0], n_pool, PAGE * A_KV, A_DIM),
                v=cache_dsa_v.reshape(cache_dsa_v.shape[0], n_pool, PAGE * A_KV, A_DIM),
                kidx=jnp.swapaxes(cache_dsa_kidx, 2, 3), ckv=cache_mla_ckv,
                kr=jnp.swapaxes(cache_mla_krope, 2, 3), fconv=state_ffn_conv,
                pt=page_table.reshape(-1), n_pages=n_pages)
    ms = bs * ts
    pos_s = n_pages * PAGE + jnp.tile(jnp.arange(ts), bs)
    cfg_s = dict(batch=bs, t=ts, tm=min(512, ms), tq_dsa=None, tq_mla=None, tm_ffn=ms, fc=256,
                 pp=min(16, n_pages), bg=min(8, bs))
    out_s = _trunk(x_sample.reshape(ms, D_MODEL), pos_s, P, W, past, cfg_s)
    return (out_p[0], out_s[0], *out_p[1:], *out_s[1:])
```

```python
import functools
import math

import jax
import jax.numpy as jnp
import numpy as np
from jax import lax
from jax.experimental import pallas as pl
from jax.experimental.pallas import tpu as pltpu

F32 = jnp.float32
BF16 = jnp.bfloat16
I32 = jnp.int32

D_MODEL = 1024
DEPTH = 4
PAGE = 128
M_HEADS, M_QK, M_V, M_CONV, M_CHUNK = 4, 64, 128, 4, 128
A_HEADS, A_KV, A_DIM = 4, 2, 128
IDX_HEADS, IDX_DIM, DSA_TOPK = 4, 64, 256
IDX_SCALE = (IDX_HEADS * IDX_DIM) ** -0.5
C_HEADS, Q_LORA, KV_LORA, NOPE, ROPE_D, C_V = 8, 256, 128, 128, 64, 128
MLA_SCALE = (NOPE + ROPE_D) ** -0.5
D_FF, FFN_CONV = 2816, 3
ROPE_THETA = 10000.0
EPS = 1e-6
AB_WIDTHS = (512, 512, 512, 4, 4, 512, 256, 256, 256, 64, 4)
AB_PACKED = 2944
INT_MIN = -(2 ** 31)
NEG = -1e30
LOG2E = 1.4426950408889634
VMEM_LIMIT = 56 * 1024 * 1024

_NT = (((1,), (1,)), ((), ()))


def _cp(sem):
    return pltpu.CompilerParams(dimension_semantics=sem, vmem_limit_bytes=VMEM_LIMIT)


def _dot(a, b):
    return jnp.dot(a, b, preferred_element_type=F32)


def _dot_nt(a, b):
    return lax.dot_general(a, b, _NT, preferred_element_type=F32)


def _rms(x, g):
    return x * lax.rsqrt(jnp.mean(x * x, axis=-1, keepdims=True) + EPS) * g


def _rope_piece(p, cos, sin, d):
    if d == 128:
        rot = pltpu.roll(p, 64, axis=1)
    else:
        lane = lax.broadcasted_iota(I32, p.shape, 1)
        rot = jnp.where((lane & 63) < 32, pltpu.roll(p, 96, axis=1), pltpu.roll(p, 32, axis=1))
    return p * cos + rot * sin


def _rope_wide(x, cos, sin, d):
    n = x.shape[1] // 128
    return jnp.concatenate([_rope_piece(x[:, i * 128:(i + 1) * 128], cos, sin, d) for i in range(n)], axis=1)


def _float_key(s):
    bits = pltpu.bitcast(s, I32)
    key = bits ^ ((bits >> 31) & 0x7FFFFFFF)
    return jnp.where(bits == INT_MIN, 0, key)


def _lane_fold(x, op):
    out = x[:, 0:128]
    for u in range(1, x.shape[1] // 128):
        out = op(out, x[:, u * 128:(u + 1) * 128])
    return out


def _topk_threshold(count, rows, k, idx_bits):
    t0 = jnp.where(count(lambda kc, i0, rs: kc >= 0) >= k, 0, INT_MIN).astype(I32)

    def bit_body(i, t):
        cand = t | jnp.left_shift(jnp.int32(1), 30 - i)
        return jnp.where(count(lambda kc, i0, rs: kc >= cand[rs]) >= k, cand, t)

    thr = lax.fori_loop(0, 31, bit_body, t0)
    n_gt = count(lambda kc, i0, rs: kc > thr[rs])
    n_eq = count(lambda kc, i0, rs: kc == thr[rs])
    need = k - n_gt
    tie = jnp.max(jnp.where((n_eq > need) & (thr > INT_MIN), 1, 0)) > 0

    def search():
        def idx_body(i, p):
            cand = p | jnp.left_shift(jnp.int32(1), idx_bits - 1 - i)

            def pred(kc, i0, rs):
                idx = i0 + lax.broadcasted_iota(I32, kc.shape, 1)
                return (kc == thr[rs]) & (idx < cand[rs])
            return jnp.where(count(pred) < need, cand, p)
        return lax.fori_loop(0, idx_bits, idx_body, jnp.zeros((rows, 1), I32))

    cut = lax.cond(tie, search, lambda: jnp.full((rows, 1), 2 ** 31 - 1, I32))
    return thr, cut


def _selected(kc, idx, thr, cut):
    return (kc > thr) | ((kc == thr) & (idx <= cut) & (thr > INT_MIN))


def _flash_init(m_ref, l_ref, acc_ref):
    m_ref[...] = jnp.full(m_ref.shape, NEG, F32)
    l_ref[...] = jnp.zeros(l_ref.shape, F32)
    acc_ref[...] = jnp.zeros(acc_ref.shape, F32)


def _flash_update(s2, v, m_ref, l_ref, acc_ref, keep=None):
    if keep is not None:
        s2 = jnp.where(keep, s2, NEG)
    m_old = m_ref[...]
    m_new = jnp.maximum(m_old, jnp.max(_lane_fold(s2, jnp.maximum), axis=1, keepdims=True))
    alpha = jnp.exp2(m_old - m_new)
    n = s2.shape[1] // 128
    tiles = [jnp.exp2(s2[:, u * 128:(u + 1) * 128] - m_new) for u in range(n)]
    if keep is not None:
        tiles = [jnp.where(keep[:, u * 128:(u + 1) * 128], t, 0.0) for u, t in enumerate(tiles)]
    lsum = tiles[0]
    for t in tiles[1:]:
        lsum = lsum + t
    l_ref[...] = alpha * l_ref[...] + lsum
    p = jnp.concatenate(tiles, axis=1).astype(BF16)
    acc_ref[...] = alpha * acc_ref[...] + _dot(p, v)
    m_ref[...] = m_new


def _flash_finish(l_ref, acc_ref):
    return acc_ref[...] / jnp.sum(l_ref[...], axis=1, keepdims=True)


def _ab_proj_kernel(x_ref, g_ref, w_ref, c128_ref, s128_ref, c64_ref, s64_ref,
                    qk_ref, v_ref, o_ref, qa_ref, ka_ref, kab_ref, va_ref, vab_ref,
                    qi_ref, ki_ref, kib_ref, gt_ref):
    h = _rms(x_ref[...], g_ref[...]).astype(BF16)

    def seg(a, b):
        return _dot(h, w_ref[:, a:b])

    qk_ref[...] = seg(0, 512)
    v_ref[...] = seg(512, 1024)
    o_ref[...] = seg(1024, 1536)
    c128, s128, c64, s64 = c128_ref[...], s128_ref[...], c64_ref[...], s64_ref[...]
    qa_ref[...] = _rope_wide(seg(1536, 2048), c128, s128, 128).astype(BF16)
    ka = _rope_wide(seg(2048, 2304), c128, s128, 128)
    ka_ref[...] = ka
    kab_ref[...] = ka.astype(BF16)
    va = seg(2304, 2560)
    va_ref[...] = va
    vab_ref[...] = va.astype(BF16)
    qi_ref[...] = _rope_wide(seg(2560, 2816), c64, s64, 64).astype(BF16)
    last = seg(2816, 2944)
    ki = _rope_piece(last, c64, s64, 64)[:, :64]
    ki_ref[...] = ki
    kib_ref[...] = ki.astype(BF16)
    gt_ref[...] = last[:, 64:]


def _ab_proj(x, g, w, tabs, tm):
    m = x.shape[0]
    c128, s128, c64, s64 = tabs
    tb = c128.shape[0] // tm
    row = lambda n: pl.BlockSpec((tm, n), lambda i: (i, 0))
    tab = pl.BlockSpec((tm, 128), lambda i: (i % tb, 0))
    full = lambda a: pl.BlockSpec(a.shape, lambda i: (0,) * a.ndim)
    widths = (512, 512, 512, 512, 256, 256, 256, 256, 256, 64, 64, 64)
    dtypes = (F32, F32, F32, BF16, F32, BF16, F32, BF16, BF16, F32, BF16, F32)
    return pl.pallas_call(
        _ab_proj_kernel,
        grid=(m // tm,),
        in_specs=[row(D_MODEL), full(g), full(w), tab, tab, tab, tab],
        out_specs=[row(n) for n in widths],
        out_shape=[jax.ShapeDtypeStruct((m, n), dt) for n, dt in zip(widths, dtypes)],
        compiler_params=_cp(("parallel",)),
    )(x, g, w, c128, s128, c64, s64)


def _conv_silu_kernel(x_ref, buf_ref, w_ref, sc_ref, o_ref, work, *, tm, hb, stride, taps):
    @pl.when(pl.program_id(1) == 0)
    def _():
        work[0:hb, :] = buf_ref[0]

    x = x_ref[...]
    work[hb:hb + tm, :] = x
    y = x * w_ref[taps - 1:taps, :]
    for j in range(taps - 1):
        y = y + work[pl.ds(hb - (taps - 1 - j) * stride, tm), :] * w_ref[j:j + 1, :]
    o_ref[...] = y * jax.nn.sigmoid(y) * sc_ref[...]
    work[0:hb, :] = x[tm - hb:, :]


def _conv_silu(x, bufp, w, scale, groups, tm, stride):
    m, c = x.shape
    hb = bufp.shape[1]
    nt = m // groups // tm
    taps = w.shape[0]
    return pl.pallas_call(
        functools.partial(_conv_silu_kernel, tm=tm, hb=hb, stride=stride, taps=taps),
        grid=(groups, nt),
        in_specs=[pl.BlockSpec((tm, c), lambda g, i: (g * nt + i, 0)),
                  pl.BlockSpec((1, hb, c), lambda g, i: (g, 0, 0)),
                  pl.BlockSpec(w.shape, lambda g, i: (0, 0)),
                  pl.BlockSpec(scale.shape, lambda g, i: (0, 0))],
        out_specs=pl.BlockSpec((tm, c), lambda g, i: (g * nt + i, 0)),
        out_shape=jax.ShapeDtypeStruct((m, c), F32),
        scratch_shapes=[pltpu.VMEM((hb + tm, c), F32)],
        compiler_params=_cp(("arbitrary", "arbitrary")),
    )(x, bufp, w, scale)


def _log_sigmoid(x):
    return jnp.minimum(x, 0.0) - jnp.log1p(jnp.exp(-jnp.abs(x)))


def _mlstm_kernel(qk_ref, v_ref, o_ref, gt_ref, gtt_ref, brow_ref, bcol_ref, gh_ref,
                  c0_ref, n0_ref, m0_ref, h_ref, c1_ref, n1_ref, m1_ref, c_s, n_s, m_s, *, cl, nc):
    c = pl.program_id(1)

    @pl.when(c == 0)
    def _():
        c_s[...] = c0_ref[0]
        for h in range(M_HEADS):
            n_s[h] = n0_ref[0, h:h + 1, :]
            m_s[h] = m0_ref[0, :, h:h + 1]

    qk = qk_ref[...]
    vv = v_ref[...]
    og = o_ref[...]
    gates = gt_ref[...] + brow_ref[...]
    gates_t = gtt_ref[0] + bcol_ref[...]
    li_col = gates[:, 0:4]
    lf_col = _log_sigmoid(gates[:, 4:8])
    li_row = gates_t[0:4, :]
    lf_row = _log_sigmoid(gates_t[4:8, :])
    r_i = lax.broadcasted_iota(I32, (cl, cl), 0)
    c_i = lax.broadcasted_iota(I32, (cl, cl), 1)
    causal = c_i <= r_i
    outs = []
    for h in range(M_HEADS):
        q = qk[:, h * M_QK:(h + 1) * M_QK]
        k = qk[:, 256 + h * M_QK:256 + (h + 1) * M_QK]
        v = vv[:, h * M_V:(h + 1) * M_V]
        cm = c_s[h]
        n = n_s[h]
        m_prev = m_s[h]
        b_col = jnp.sum(jnp.where(causal, lf_row[h:h + 1, :], 0.0), axis=1, keepdims=True)
        b_row = jnp.sum(jnp.where(r_i <= c_i, lf_col[:, h:h + 1], 0.0), axis=0, keepdims=True)
        dmat = jnp.where(causal, b_col - b_row + li_row[h:h + 1, :], -jnp.inf)
        inter = b_col + m_prev
        m_t = jnp.maximum(inter, jnp.max(dmat, axis=1, keepdims=True))
        iw = jnp.exp(inter - m_t)
        qb, kb, vb = q.astype(BF16), k.astype(BF16), v.astype(BF16)
        s = _dot_nt(qb, kb) * jnp.exp(dmat - m_t)
        num = iw * _dot(qb, cm.astype(BF16)) + _dot(s.astype(BF16), vb)
        den = iw * jnp.sum(q * n, axis=1, keepdims=True) + jnp.sum(s, axis=1, keepdims=True)
        hh = num / jnp.maximum(jnp.abs(den), jnp.exp(-m_t))
        m_new = m_t[cl - 1:cl, :]
        b_last = b_col[cl - 1:cl, :]
        w_end = jnp.exp(b_last - b_col + li_col[:, h:h + 1] - m_new)
        decay = jnp.exp(b_last + m_prev - m_new)
        kw = k * w_end
        c_s[h] = decay * cm + lax.dot_general(kw.astype(BF16), vb, (((0,), (0,)), ((), ())),
                                              preferred_element_type=F32)
        n_s[h] = decay * n + jnp.sum(kw, axis=0, keepdims=True)
        m_s[h] = m_new
        hn = _rms(hh, gh_ref[:, h * M_V:(h + 1) * M_V])
        outs.append(hn * jax.nn.sigmoid(og[:, h * M_V:(h + 1) * M_V]))
    h_ref[...] = jnp.concatenate(outs, axis=1)

    @pl.when(c == nc - 1)
    def _():
        c1_ref[0] = c_s[...]
        for h in range(M_HEADS):
            n1_ref[0, h:h + 1, :] = n_s[h]
            m1_ref[0, :, h:h + 1] = m_s[h]


def _mlstm(qkc, v, o, gates, gates_t, b_row, b_col, g_head, c0, n0, m0, batch, cl):
    m = qkc.shape[0]
    nc = m // batch // cl
    row = lambda n: pl.BlockSpec((cl, n), lambda b, c: (b * nc + c, 0))
    full = lambda a: pl.BlockSpec(a.shape, lambda b, c: (0,) * a.ndim)
    st_c = pl.BlockSpec((1, M_HEADS, M_QK, M_V), lambda b, c: (b, 0, 0, 0))
    st_n = pl.BlockSpec((1, M_HEADS, M_QK), lambda b, c: (b, 0, 0))
    st_m = pl.BlockSpec((1, 1, M_HEADS), lambda b, c: (b, 0, 0))
    return pl.pallas_call(
        functools.partial(_mlstm_kernel, cl=cl, nc=nc),
        grid=(batch, nc),
        in_specs=[row(512), row(512), row(512), row(64),
                  pl.BlockSpec((1, 8, cl), lambda b, c: (b, 0, c)),
                  full(b_row), full(b_col), full(g_head), st_c, st_n, st_m],
        out_specs=[row(512), st_c, st_n, st_m],
        out_shape=[jax.ShapeDtypeStruct((m, 512), F32),
                   jax.ShapeDtypeStruct((batch, M_HEADS, M_QK, M_V), F32),
                   jax.ShapeDtypeStruct((batch, M_HEADS, M_QK), F32),
                   jax.ShapeDtypeStruct((batch, 1, M_HEADS), F32)],
        scratch_shapes=[pltpu.VMEM((M_HEADS, M_QK, M_V), F32),
                        pltpu.VMEM((M_HEADS, 1, M_QK), F32),
                        pltpu.VMEM((M_HEADS, 1, 1), F32)],
        compiler_params=_cp(("arbitrary", "arbitrary")),
    )(qkc, v, o, gates, gates_t, b_row, b_col, g_head, c0, n0, m0)


def _dsa_prompt_kernel(qa_ref, qi_ref, gt_ref, kab_ref, vab_ref, kib_ref, o_ref,
                       keys, q_s, m_s, l_s, acc_s, *, tq, cw, topk, idx_bits):
    i = pl.program_id(1)
    n_ch = (i * tq + tq + cw - 1) // cw
    qpos = i * tq + lax.broadcasted_iota(I32, (tq, 1), 0)
    wi = gt_ref[:, 8:12] * IDX_SCALE
    qi = qi_ref[...]

    def score_body(c, carry):
        off = pl.multiple_of(c * cw, cw)
        kc = kib_ref[pl.ds(off, cw), :]
        sc = jnp.zeros((tq, cw), F32)
        for h in range(IDX_HEADS):
            rel = jnp.maximum(_dot_nt(qi[:, h * IDX_DIM:(h + 1) * IDX_DIM], kc), 0.0)
            sc = sc + rel * wi[:, h:h + 1]
        kpos = off + lax.broadcasted_iota(I32, (tq, cw), 1)
        keys[c] = jnp.where(kpos <= qpos, _float_key(sc), INT_MIN)
        return carry

    lax.fori_loop(0, n_ch, score_body, 0)

    def count(pred):
        rb = min(128, tq)
        parts = []
        for r0 in range(0, tq, rb):
            rs = slice(r0, r0 + rb)

            def body(c, acc, rs=rs):
                return acc + _lane_fold(pred(keys[c, rs, :], c * cw, rs).astype(F32), jnp.add)
            parts.append(lax.fori_loop(0, n_ch, body, jnp.zeros((rb, 128), F32)))
        return jnp.sum(jnp.concatenate(parts, axis=0), axis=1, keepdims=True)

    thr, cut = _topk_threshold(count, tq, topk, idx_bits)

    qa = qa_ref[...]
    for g in range(A_KV):
        q_s[g] = jnp.concatenate([qa[:, (2 * g) * A_DIM:(2 * g + 1) * A_DIM],
                                  qa[:, (2 * g + 1) * A_DIM:(2 * g + 2) * A_DIM]], axis=0)
        _flash_init(m_s.at[g], l_s.at[g], acc_s.at[g])

    def att_body(c, carry):
        off = pl.multiple_of(c * cw, cw)
        idx = off + lax.broadcasted_iota(I32, (tq, cw), 1)
        bias = jnp.where(_selected(keys[c], idx, thr, cut), 0.0, NEG)
        bias2 = jnp.concatenate([bias, bias], axis=0)
        for g in range(A_KV):
            kg = kab_ref[pl.ds(off, cw), g * A_DIM:(g + 1) * A_DIM]
            vg = vab_ref[pl.ds(off, cw), g * A_DIM:(g + 1) * A_DIM]
            s2 = _dot_nt(q_s[g], kg) * (A_DIM ** -0.5 * LOG2E) + bias2
            _flash_update(s2, vg, m_s.at[g], l_s.at[g], acc_s.at[g])
        return carry

    lax.fori_loop(0, n_ch, att_body, 0)
    outs = []
    for g in range(A_KV):
        og = _flash_finish(l_s.at[g], acc_s.at[g])
        outs += [og[0:tq], og[tq:2 * tq]]
    o_ref[...] = jnp.concatenate(outs, axis=1)


def _dsa_prompt(qa, qi, gates, kab, vab, kib, batch, topk, tq):
    m = qa.shape[0]
    t = m // batch
    nq = t // tq
    cw = min(512, t)
    row = lambda n: pl.BlockSpec((tq, n), lambda b, i: (b * nq + i, 0))
    whole = lambda n: pl.BlockSpec((t, n), lambda b, i: (b, 0))
    return pl.pallas_call(
        functools.partial(_dsa_prompt_kernel, tq=tq, cw=cw, topk=topk,
                          idx_bits=max(1, (t - 1).bit_length())),
        grid=(batch, nq),
        in_specs=[row(512), row(256), row(64), whole(256), whole(256), whole(64)],
        out_specs=row(512),
        out_shape=jax.ShapeDtypeStruct((m, 512), F32),
        scratch_shapes=[pltpu.VMEM((t // cw, tq, cw), I32),
                        pltpu.VMEM((A_KV, 2 * tq, A_DIM), BF16),
                        pltpu.VMEM((A_KV, 2 * tq, 128), F32),
                        pltpu.VMEM((A_KV, 2 * tq, 128), F32),
                        pltpu.VMEM((A_KV, 2 * tq, A_DIM), F32)],
        compiler_params=_cp(("parallel", "arbitrary")),
    )(qa, qi, gates, kab, vab, kib)


def _idx_scores(qi, gates, keys_mat, keys_on_lanes):
    qs = jnp.concatenate([qi[:, h * IDX_DIM:(h + 1) * IDX_DIM] for h in range(IDX_HEADS)], axis=0)
    ws = jnp.concatenate([gates[:, 8 + h:9 + h] for h in range(IDX_HEADS)], axis=0) * IDX_SCALE
    qk = _dot(qs, keys_mat) if keys_on_lanes else _dot_nt(qs, keys_mat)
    rel = jnp.maximum(qk, 0.0) * ws
    t = qi.shape[0]
    sc = rel[0:t]
    for h in range(1, IDX_HEADS):
        sc = sc + rel[h * t:(h + 1) * t]
    return sc


def _dsa_scores_sample_kernel(pt_ref, qi_ref, gt_ref, knew_ref, *rest, pp):
    pages = rest[:pp]
    kp_ref, kn_ref = rest[pp], rest[pp + 1]
    qi = qi_ref[...]
    gates = gt_ref[...]
    kcat = jnp.concatenate([p[...] for p in pages], axis=1).astype(BF16)
    kp_ref[0] = _float_key(_idx_scores(qi, gates, kcat, True))
    sn = _idx_scores(qi, gates, knew_ref[0], False)
    t = qi.shape[0]
    vis = lax.broadcasted_iota(I32, (t, PAGE), 1) <= lax.broadcasted_iota(I32, (t, PAGE), 0)
    kn_ref[0] = jnp.where(vis, _float_key(sn), INT_MIN)


def _page_specs(shape_tail, layer, n_pages, pp):
    nd = len(shape_tail)

    def spec(u):
        return pl.BlockSpec((None, None) + shape_tail,
                            lambda b, j, pt: (layer, pt[b * n_pages + j * pp + u]) + (0,) * nd)
    return [spec(u) for u in range(pp)]


def _dsa_scores_sample(pt, qi, gates, knew, pool, layer, batch, tt, n_pages, pp):
    grid_spec = pltpu.PrefetchScalarGridSpec(
        num_scalar_prefetch=1,
        grid=(batch, n_pages // pp),
        in_specs=[pl.BlockSpec((tt, 256), lambda b, j, pt: (b, 0)),
                  pl.BlockSpec((tt, 64), lambda b, j, pt: (b, 0)),
                  pl.BlockSpec((1, PAGE, IDX_DIM), lambda b, j, pt: (b, 0, 0))]
                 + _page_specs((IDX_DIM, PAGE), layer, n_pages, pp),
        out_specs=[pl.BlockSpec((1, tt, pp * PAGE), lambda b, j, pt: (b, 0, j)),
                   pl.BlockSpec((1, tt, PAGE), lambda b, j, pt: (b, 0, 0))],
    )
    return pl.pallas_call(
        functools.partial(_dsa_scores_sample_kernel, pp=pp),
        grid_spec=grid_spec,
        out_shape=[jax.ShapeDtypeStruct((batch, tt, n_pages * PAGE), I32),
                   jax.ShapeDtypeStruct((batch, tt, PAGE), I32)],
        compiler_params=_cp(("parallel", "arbitrary")),
    )(pt, qi, gates, knew, *([pool] * pp))


def _thresh_sample_kernel(kp_ref, kn_ref, thr_ref, cut_ref, *, rows, cw, n_past, topk, idx_bits):
    def count(pred):
        tt = kp_ref.shape[1]
        rb = min(128, rows)
        parts = []
        for r0 in range(0, rows, rb):
            rs = slice(r0, r0 + rb)
            bs = slice(r0 // tt, (r0 + rb) // tt)
            acc = pred(kn_ref[bs].reshape(rb, PAGE), n_past * cw, rs).astype(F32)
            for c in range(n_past):
                kc = kp_ref[bs, :, c * cw:(c + 1) * cw].reshape(rb, cw)
                acc = acc + _lane_fold(pred(kc, c * cw, rs).astype(F32), jnp.add)
            parts.append(acc)
        return jnp.sum(jnp.concatenate(parts, axis=0), axis=1, keepdims=True)

    thr, cut = _topk_threshold(count, rows, topk, idx_bits)
    thr_ref[...] = jnp.broadcast_to(thr, (rows, PAGE)).reshape(thr_ref.shape)
    cut_ref[...] = jnp.broadcast_to(cut, (rows, PAGE)).reshape(cut_ref.shape)


def _thresh_sample(keys_past, keys_new, topk, bg):
    batch, tt, lp = keys_past.shape
    cw = 512
    rows = bg * tt
    n_past = lp // cw
    blk = lambda n: pl.BlockSpec((bg, tt, n), lambda i: (i, 0, 0))
    return pl.pallas_call(
        functools.partial(_thresh_sample_kernel, rows=rows, cw=cw, n_past=n_past, topk=topk,
                          idx_bits=(lp + PAGE - 1).bit_length()),
        grid=(batch // bg,),
        in_specs=[blk(lp), blk(PAGE)],
        out_specs=[blk(PAGE), blk(PAGE)],
        out_shape=[jax.ShapeDtypeStruct((batch, tt, PAGE), I32)] * 2,
        compiler_params=_cp(("parallel",)),
    )(keys_past, keys_new)


def _dsa_attend_sample_kernel(pt_ref, qa_ref, thr_ref, cut_ref, kp_ref, kn_ref, knew_ref, vnew_ref, *rest,
                              pp, n_steps, n_past_keys):
    kpages, vpages = rest[:pp], rest[pp:2 * pp]
    o_ref = rest[2 * pp]
    m_s, l_s, acc_s = rest[2 * pp + 1:]
    j = pl.program_id(1)
    tt = qa_ref.shape[0]

    @pl.when(j == 0)
    def _():
        for g in range(A_KV):
            _flash_init(m_s.at[g], l_s.at[g], acc_s.at[g])

    qa = qa_ref[...]
    qg = [jnp.concatenate([qa[:, (2 * g) * A_DIM:(2 * g + 1) * A_DIM],
                           qa[:, (2 * g + 1) * A_DIM:(2 * g + 2) * A_DIM]], axis=0) for g in range(A_KV)]
    thr = thr_ref[0][:, 0:1]
    cut = cut_ref[0][:, 0:1]

    def attend(kc, first_idx, kv_of_group):
        idx = first_idx + lax.broadcasted_iota(I32, kc.shape, 1)
        bias = jnp.where(_selected(kc, idx, thr, cut), 0.0, NEG)
        bias2 = jnp.concatenate([bias, bias], axis=0)
        for g in range(A_KV):
            kg, vg = kv_of_group(g)
            s2 = _dot_nt(qg[g], kg) * (A_DIM ** -0.5 * LOG2E) + bias2
            _flash_update(s2, vg, m_s.at[g], l_s.at[g], acc_s.at[g])

    def paged(g):
        rows = lambda p: p[pl.ds(g, PAGE, stride=A_KV), :]
        return (jnp.concatenate([rows(p) for p in kpages], axis=0).astype(BF16),
                jnp.concatenate([rows(p) for p in vpages], axis=0).astype(BF16))

    attend(kp_ref[0], j * (pp * PAGE), paged)

    @pl.when(j == n_steps - 1)
    def _():
        attend(kn_ref[0], n_past_keys,
               lambda g: (knew_ref[0][:, g * A_DIM:(g + 1) * A_DIM], vnew_ref[0][:, g * A_DIM:(g + 1) * A_DIM]))
        outs = []
        for g in range(A_KV):
            og = _flash_finish(l_s.at[g], acc_s.at[g])
            outs += [og[0:tt], og[tt:2 * tt]]
        o_ref[...] = jnp.concatenate(outs, axis=1)


def _dsa_attend_sample(pt, qa, thr, cut, keys_past, keys_new, knew, vnew, pool_k, pool_v,
                       layer, batch, tt, n_pages, pp):
    n_steps = n_pages // pp
    kvw = A_KV * A_DIM
    grid_spec = pltpu.PrefetchScalarGridSpec(
        num_scalar_prefetch=1,
        grid=(batch, n_steps),
        in_specs=[pl.BlockSpec((tt, 512), lambda b, j, pt: (b, 0)),
                  pl.BlockSpec((1, tt, PAGE), lambda b, j, pt: (b, 0, 0)),
                  pl.BlockSpec((1, tt, PAGE), lambda b, j, pt: (b, 0, 0)),
                  pl.BlockSpec((1, tt, pp * PAGE), lambda b, j, pt: (b, 0, j)),
                  pl.BlockSpec((1, tt, PAGE), lambda b, j, pt: (b, 0, 0)),
                  pl.BlockSpec((1, PAGE, kvw), lambda b, j, pt: (b, 0, 0)),
                  pl.BlockSpec((1, PAGE, kvw), lambda b, j, pt: (b, 0, 0))]
                 + _page_specs((PAGE * A_KV, A_DIM), layer, n_pages, pp)
                 + _page_specs((PAGE * A_KV, A_DIM), layer, n_pages, pp),
        out_specs=pl.BlockSpec((tt, 512), lambda b, j, pt: (b, 0)),
        scratch_shapes=[pltpu.VMEM((A_KV, 2 * tt, 128), F32),
                        pltpu.VMEM((A_KV, 2 * tt, 128), F32),
                        pltpu.VMEM((A_KV, 2 * tt, A_DIM), F32)],
    )
    return pl.pallas_call(
        functools.partial(_dsa_attend_sample_kernel, pp=pp, n_steps=n_steps, n_past_keys=n_pages * PAGE),
        grid_spec=grid_spec,
        out_shape=jax.ShapeDtypeStruct((batch * tt, 512), F32),
        compiler_params=_cp(("parallel", "arbitrary")),
    )(pt, qa, thr, cut, keys_past, keys_new, knew, vnew, *([pool_k] * pp), *([pool_v] * pp))


def _mla_pre_kernel(x_ref, g_ref, win_ref, gq_ref, gkv_ref, wuq_ref, wuk_ref, c64_ref, s64_ref,
                    ckv_ref, kr_ref, kcat_ref, qcat_ref):
    h = _rms(x_ref[...], g_ref[...]).astype(BF16)
    cq = _rms(_dot(h, win_ref[:, 0:Q_LORA]), gq_ref[...])
    rest = _dot(h, win_ref[:, Q_LORA:Q_LORA + 256])
    ckv = _rms(rest[:, 0:KV_LORA], gkv_ref[...])
    ckv_ref[...] = ckv
    c64, s64 = c64_ref[...], s64_ref[...]
    lane = lax.broadcasted_iota(I32, (h.shape[0], 128), 1)
    kr = jnp.where(lane < ROPE_D, _rope_piece(rest[:, 128:256], c64, s64, 64), 0.0)
    kr_ref[...] = kr[:, :ROPE_D]
    kcat_ref[...] = jnp.concatenate([ckv, kr], axis=1).astype(BF16)
    q = _dot(cq.astype(BF16), wuq_ref[...])
    qr = _rope_wide(q[:, 1024:1536], c64, s64, 64)
    pieces = []
    for hh in range(C_HEADS):
        pieces.append(_dot(q[:, hh * NOPE:(hh + 1) * NOPE].astype(BF16), wuk_ref[hh]))
        pair = qr[:, (hh // 2) * 128:(hh // 2 + 1) * 128]
        if hh % 2:
            pair = pltpu.roll(pair, ROPE_D, axis=1)
        pieces.append(jnp.where(lane < ROPE_D, pair, 0.0))
    qcat_ref[...] = jnp.concatenate(pieces, axis=1).astype(BF16)


def _mla_pre(x, g, win, gq, gkv, wuq, wuk, tabs, tm):
    m = x.shape[0]
    c64, s64 = tabs[2], tabs[3]
    tb = c64.shape[0] // tm
    row = lambda n: pl.BlockSpec((tm, n), lambda i: (i, 0))
    tab = pl.BlockSpec((tm, 128), lambda i: (i % tb, 0))
    full = lambda a: pl.BlockSpec(a.shape, lambda i: (0,) * a.ndim)
    widths = (KV_LORA, ROPE_D, 256, C_HEADS * 256)
    dtypes = (F32, F32, BF16, BF16)
    return pl.pallas_call(
        _mla_pre_kernel,
        grid=(m // tm,),
        in_specs=[row(D_MODEL), full(g), full(win), full(gq), full(gkv), full(wuq), full(wuk), tab, tab],
        out_specs=[row(n) for n in widths],
        out_shape=[jax.ShapeDtypeStruct((m, n), dt) for n, dt in zip(widths, dtypes)],
        compiler_params=_cp(("parallel",)),
    )(x, g, win, gq, gkv, wuq, wuk, c64, s64)


def _stack_heads(qcat):
    return jnp.concatenate([qcat[:, h * 256:(h + 1) * 256] for h in range(C_HEADS)], axis=0)


def _unstack_heads(o, t):
    return jnp.concatenate([o[h * t:(h + 1) * t] for h in range(C_HEADS)], axis=1)


def _mla_prompt_kernel(q_ref, k_ref, o_ref, q_s, m_s, l_s, acc_s, *, tq):
    i = pl.program_id(1)
    q_s[...] = _stack_heads(q_ref[...])
    _flash_init(m_s, l_s, acc_s)

    def step(c, keep):
        kc = k_ref[pl.ds(pl.multiple_of(c * tq, tq), tq), :]
        s2 = _dot_nt(q_s[...], kc) * (MLA_SCALE * LOG2E)
        _flash_update(s2, kc[:, 0:KV_LORA], m_s, l_s, acc_s, keep=keep)

    def body(c, carry):
        step(c, None)
        return carry

    lax.fori_loop(0, i, body, 0)
    rows = C_HEADS * tq
    t_in = lax.broadcasted_iota(I32, (rows, tq), 0) & (tq - 1)
    step(i, lax.broadcasted_iota(I32, (rows, tq), 1) <= t_in)
    o_ref[...] = _unstack_heads(_flash_finish(l_s, acc_s), tq).astype(BF16)


def _mla_prompt(qcat, kcat, batch, tq):
    m = qcat.shape[0]
    t = m // batch
    nq = t // tq
    return pl.pallas_call(
        functools.partial(_mla_prompt_kernel, tq=tq),
        grid=(batch, nq),
        in_specs=[pl.BlockSpec((tq, C_HEADS * 256), lambda b, i: (b * nq + i, 0)),
                  pl.BlockSpec((t, 256), lambda b, i: (b, 0))],
        out_specs=pl.BlockSpec((tq, C_HEADS * KV_LORA), lambda b, i: (b * nq + i, 0)),
        out_shape=jax.ShapeDtypeStruct((m, C_HEADS * KV_LORA), BF16),
        scratch_shapes=[pltpu.VMEM((C_HEADS * tq, 256), BF16),
                        pltpu.VMEM((C_HEADS * tq, 128), F32),
                        pltpu.VMEM((C_HEADS * tq, 128), F32),
                        pltpu.VMEM((C_HEADS * tq, KV_LORA), F32)],
        compiler_params=_cp(("parallel", "arbitrary")),
    )(qcat, kcat)


def _mla_sample_kernel(pt_ref, q_ref, knew_ref, *rest, pp, n_steps):
    cpages, rpages = rest[:pp], rest[pp:2 * pp]
    o_ref = rest[2 * pp]
    m_s, l_s, acc_s = rest[2 * pp + 1:]
    j = pl.program_id(1)
    tt = q_ref.shape[0]

    @pl.when(j == 0)
    def _():
        _flash_init(m_s, l_s, acc_s)

    qs = _stack_heads(q_ref[...])
    ccat = jnp.concatenate([p[...] for p in cpages], axis=0).astype(BF16)
    rcat = jnp.concatenate([p[...] for p in rpages], axis=1).astype(BF16)
    s2 = (_dot_nt(qs[:, 0:KV_LORA], ccat) + _dot(qs[:, KV_LORA:KV_LORA + ROPE_D], rcat)) * (MLA_SCALE * LOG2E)
    _flash_update(s2, ccat, m_s, l_s, acc_s)

    @pl.when(j == n_steps - 1)
    def _():
        kn = knew_ref[0]
        rows = C_HEADS * tt
        t_in = lax.broadcasted_iota(I32, (rows, PAGE), 0) & (tt - 1)
        keep = lax.broadcasted_iota(I32, (rows, PAGE), 1) <= t_in
        _flash_update(_dot_nt(qs, kn) * (MLA_SCALE * LOG2E), kn[:, 0:KV_LORA], m_s, l_s, acc_s, keep=keep)
        o_ref[...] = _unstack_heads(_flash_finish(l_s, acc_s), tt).astype(BF16)


def _mla_sample(pt, qcat, knew, pool_c, pool_r, layer, batch, tt, n_pages, pp):
    n_steps = n_pages // pp
    grid_spec = pltpu.PrefetchScalarGridSpec(
        num_scalar_prefetch=1,
        grid=(batch, n_steps),
        in_specs=[pl.BlockSpec((tt, C_HEADS * 256), lambda b, j, pt: (b, 0)),
                  pl.BlockSpec((1, PAGE, 256), lambda b, j, pt: (b, 0, 0))]
                 + _page_specs((PAGE, KV_LORA), layer, n_pages, pp)
                 + _page_specs((ROPE_D, PAGE), layer, n_pages, pp),
        out_specs=pl.BlockSpec((tt, C_HEADS * KV_LORA), lambda b, j, pt: (b, 0)),
        scratch_shapes=[pltpu.VMEM((C_HEADS * tt, 128), F32),
                        pltpu.VMEM((C_HEADS * tt, 128), F32),
                        pltpu.VMEM((C_HEADS * tt, KV_LORA), F32)],
    )
    return pl.pallas_call(
        functools.partial(_mla_sample_kernel, pp=pp, n_steps=n_steps),
        grid_spec=grid_spec,
        out_shape=jax.ShapeDtypeStruct((batch * tt, C_HEADS * KV_LORA), BF16),
        compiler_params=_cp(("parallel", "arbitrary")),
    )(pt, qcat, knew, *([pool_c] * pp), *([pool_r] * pp))


def _ab_out_kernel(x_ref, hm_ref, oa_ref, w_ref, o_ref):
    o_ref[...] = (x_ref[...] + _dot(hm_ref[...].astype(BF16), w_ref[0:512, :])
                  + _dot(oa_ref[...].astype(BF16), w_ref[512:1024, :]))


def _ab_out(x, hm, oa, w, tm):
    m = x.shape[0]
    row = lambda n: pl.BlockSpec((tm, n), lambda i: (i, 0))
    return pl.pallas_call(
        _ab_out_kernel,
        grid=(m // tm,),
        in_specs=[row(D_MODEL), row(512), row(512), pl.BlockSpec(w.shape, lambda i: (0, 0))],
        out_specs=row(D_MODEL),
        out_shape=jax.ShapeDtypeStruct((m, D_MODEL), F32),
        compiler_params=_cp(("parallel",)),
    )(x, hm, oa, w)


def _mla_out_kernel(x_ref, ol_ref, wuv_ref, w_ref, o_ref):
    ol = ol_ref[...]
    o = jnp.concatenate([_dot(ol[:, h * KV_LORA:(h + 1) * KV_LORA], wuv_ref[h]) for h in range(C_HEADS)],
                        axis=1)
    o_ref[...] = x_ref[...] + _dot(o.astype(BF16), w_ref[...])


def _mla_out(x, ol, wuv, w, tm):
    m = x.shape[0]
    row = lambda n: pl.BlockSpec((tm, n), lambda i: (i, 0))
    return pl.pallas_call(
        _mla_out_kernel,
        grid=(m // tm,),
        in_specs=[row(D_MODEL), row(1024), pl.BlockSpec(wuv.shape, lambda i: (0, 0, 0)),
                  pl.BlockSpec(w.shape, lambda i: (0, 0))],
        out_specs=row(D_MODEL),
        out_shape=jax.ShapeDtypeStruct((m, D_MODEL), F32),
        compiler_params=_cp(("parallel",)),
    )(x, ol, wuv, w)


def _ffn_kernel(x_ref, g_ref, wa_ref, wg_ref, wc_ref, bc_ref, wd_ref, buf_ref, gf_ref,
                o_ref, st_ref, y_ref, h_s, acc_s, halo, work, *, tm, hb, stride, nj, final):
    i = pl.program_id(1)
    j = pl.program_id(2)

    @pl.when(j == 0)
    def _():
        h_s[...] = _rms(x_ref[...], g_ref[...]).astype(BF16)
        acc_s[...] = jnp.zeros(acc_s.shape, F32)

    @pl.when(i == 0)
    def _():
        halo[j] = buf_ref[0]

    h = h_s[...]
    a = _dot(h, wa_ref[...])
    gg = _dot(h, wg_ref[...])
    work[0:hb, :] = halo[j]
    work[hb:hb + tm, :] = gg
    gc = gg * wc_ref[FFN_CONV - 1:FFN_CONV, :]
    for t in range(FFN_CONV - 1):
        gc = gc + work[pl.ds(hb - (FFN_CONV - 1 - t) * stride, tm), :] * wc_ref[t:t + 1, :]
    last = gg[tm - hb:, :]
    halo[j] = last
    st_ref[0, 0] = last
    gc = gc + bc_ref[...]
    p = a * (gc * jax.nn.sigmoid(gc))
    acc_s[...] += _dot(p.astype(BF16), wd_ref[...])

    @pl.when(j == nj - 1)
    def _():
        xn = x_ref[...] + acc_s[...]
        o_ref[...] = xn
        if final:
            y_ref[...] = _rms(xn, gf_ref[...])


def _ffn(x, g, wa, wg, wc, bc, wd, bufp, g_final, groups, tm, fc, stride, final):
    m = x.shape[0]
    hb = bufp.shape[1]
    nt = m // groups // tm
    nj = D_FF // fc
    row = pl.BlockSpec((tm, D_MODEL), lambda gi, i, j: (gi * nt + i, 0))
    vec = lambda a: pl.BlockSpec(a.shape, lambda gi, i, j: (0, 0))
    st = pl.BlockSpec((1, hb, fc), lambda gi, i, j: (gi, 0, j))
    st_out = pl.BlockSpec((1, 1, hb, fc), lambda gi, i, j: (gi, i, 0, j))
    outs = [row, st_out] + ([row] if final else [])
    shapes = [jax.ShapeDtypeStruct((m, D_MODEL), F32), jax.ShapeDtypeStruct((groups, nt, hb, D_FF), F32)]
    if final:
        shapes.append(jax.ShapeDtypeStruct((m, D_MODEL), F32))

    def kern(*refs):
        if final:
            return _ffn_kernel(*refs, tm=tm, hb=hb, stride=stride, nj=nj, final=True)
        ins, rest = refs[:9], refs[9:]
        return _ffn_kernel(*ins, rest[0], rest[1], None, *rest[2:], tm=tm, hb=hb, stride=stride, nj=nj,
                           final=False)

    return pl.pallas_call(
        kern,
        grid=(groups, nt, nj),
        in_specs=[row, vec(g),
                  pl.BlockSpec((D_MODEL, fc), lambda gi, i, j: (0, j)),
                  pl.BlockSpec((D_MODEL, fc), lambda gi, i, j: (0, j)),
                  pl.BlockSpec((FFN_CONV, fc), lambda gi, i, j: (0, j)),
                  pl.BlockSpec((1, fc), lambda gi, i, j: (0, j)),
                  pl.BlockSpec((fc, D_MODEL), lambda gi, i, j: (j, 0)),
                  st, vec(g_final)],
        out_specs=outs,
        out_shape=shapes,
        scratch_shapes=[pltpu.VMEM((tm, D_MODEL), BF16),
                        pltpu.VMEM((tm, D_MODEL), F32),
                        pltpu.VMEM((nj, hb, fc), F32),
                        pltpu.VMEM((hb + tm, fc), F32)],
        compiler_params=_cp(("arbitrary", "arbitrary", "arbitrary")),
    )(x, g, wa, wg, wc, bc, wd, bufp, g_final)


def _rope_tables(pos, d):
    half = d // 2
    inv = ROPE_THETA ** (-jnp.arange(half, dtype=F32) * (2.0 / d))
    ang = pos.astype(F32)[:, None] * inv[None, :]
    cos, sin = jnp.cos(ang), jnp.sin(ang)
    reps = 128 // d
    return (jnp.tile(jnp.concatenate([cos, cos], axis=1), (1, reps)),
            jnp.tile(jnp.concatenate([-sin, sin], axis=1), (1, reps)))


def _prep_weights(w_in_ab, w_out_ab, w_in_mla, w_uq, w_uk, w_uv, w_out_mla, w_up, w_down):
    cuts = np.cumsum((0,) + AB_WIDTHS)
    seg = lambda w, k: w[:, :, cuts[k]:cuts[k + 1]]
    n_ab = w_in_ab.shape[0]
    pad = jnp.zeros((n_ab, D_MODEL, 52), F32)
    w_ab = jnp.concatenate([seg(w_in_ab, 0), seg(w_in_ab, 1), seg(w_in_ab, 2), seg(w_in_ab, 5),
                            seg(w_in_ab, 6), seg(w_in_ab, 7), seg(w_in_ab, 8), seg(w_in_ab, 9),
                            seg(w_in_ab, 3), seg(w_in_ab, 4), seg(w_in_ab, 10), pad], axis=2).astype(BF16)
    n_c = w_in_mla.shape[0]
    w_mla = jnp.concatenate([w_in_mla, jnp.zeros((n_c, D_MODEL, 64), F32)], axis=2).astype(BF16)
    uq = w_uq.reshape(n_c, Q_LORA, C_HEADS, NOPE + ROPE_D)
    w_uq2 = jnp.concatenate([uq[..., :NOPE].reshape(n_c, Q_LORA, C_HEADS * NOPE),
                             uq[..., NOPE:].reshape(n_c, Q_LORA, C_HEADS * ROPE_D)], axis=2).astype(BF16)
    return dict(w_ab=w_ab, w_out_ab=w_out_ab.astype(BF16), w_mla=w_mla, w_uq=w_uq2,
                w_uk=w_uk.astype(BF16), w_uv=w_uv.astype(BF16), w_out_mla=w_out_mla.astype(BF16),
                w_a=w_up[:, :, :D_FF].astype(BF16), w_g=w_up[:, :, D_FF:].astype(BF16),
                w_down=w_down.astype(BF16))


def _front_pad(buf, hb):
    g, r, c = buf.shape
    return jnp.concatenate([jnp.zeros((g, hb - r, c), F32), buf], axis=1)


def _trunk(x, pos_rows, P, W, past, cfg):
    batch, tt = cfg["batch"], cfg["t"]
    tm, tq_dsa, tq_mla = cfg["tm"], cfg["tq_dsa"], cfg["tq_mla"]
    m = batch * tt
    c128, s128 = _rope_tables(pos_rows, 128)
    c64, s64 = _rope_tables(pos_rows, 64)
    tabs = (c128, s128, c64, s64)
    kscale = jnp.concatenate([jnp.ones((1, 256), F32), jnp.full((1, 256), M_QK ** -0.5, F32)], axis=1)
    ab_states, c_states, ffn_states = [], [], []
    y_final = None
    for l in range(DEPTH):
        j = l // 2
        g_attn = P["g_attn"][l][None, :]
        if l % 2 == 0:
            (qk_m, v_m, o_m, qa, ka, kab, va, vab, qi, ki, kib, gates) = _ab_proj(x, g_attn, W["w_ab"][j], tabs, tm)
            if past is None:
                conv_buf = jnp.zeros((batch, M_CONV - 1, 512), F32)
                c0 = jnp.zeros((batch, M_HEADS, M_QK, M_V), F32)
                n0 = jnp.zeros((batch, M_HEADS, M_QK), F32)
                m0 = jnp.zeros((batch, 1, M_HEADS), F32)
            else:
                conv_buf = past["mconv"][j]
                c0, n0, m0 = past["C"][j], past["n"][j], past["m"][j][:, None, :]
            qkc = _conv_silu(qk_m, _front_pad(conv_buf, 8), P["w_mconv"][j], kscale,
                             groups=batch, tm=min(tm, tt), stride=1)
            cl = math.gcd(tt, M_CHUNK)
            gates_t = jnp.transpose(gates[:, :8].reshape(batch, tt, 8), (0, 2, 1))
            bias = jnp.concatenate([P["b_igate"][j], P["b_fgate"][j]])
            b_row = jnp.concatenate([bias, jnp.zeros((56,), F32)])[None, :]
            h_m, c1, n1, m1 = _mlstm(qkc, v_m, o_m, gates, gates_t, b_row, bias[:, None],
                                     P["g_mhead"][j].reshape(1, 512), c0, n0, m0, batch, cl)
            if past is None:
                o_a = _dsa_prompt(qa, qi, gates, kab, vab, kib, batch, min(DSA_TOPK, tt // 4), tq_dsa)
            else:
                pt, n_pages, pp = past["pt"], past["n_pages"], cfg["pp"]
                pad_page = lambda a: jnp.pad(a.reshape(batch, tt, -1), ((0, 0), (0, PAGE - tt), (0, 0)))
                keys_past, keys_new = _dsa_scores_sample(pt, qi, gates, pad_page(kib), past["kidx"], j,
                                                         batch, tt, n_pages, cfg["pp_idx"])
                topk = min(DSA_TOPK, (n_pages * PAGE + tt) // 4)
                thr, cut = _thresh_sample(keys_past, keys_new, topk, cfg["bg"])
                o_a = _dsa_attend_sample(pt, qa, thr, cut, keys_past, keys_new, pad_page(kab), pad_page(vab),
                                         past["k"], past["v"], j, batch, tt, n_pages, pp)
            x = _ab_out(x, h_m, o_a, W["w_out_ab"][j], tm)
            ab_states.append((c1, n1, m1.reshape(batch, M_HEADS),
                              qk_m.reshape(batch, tt, 512)[:, tt - (M_CONV - 1):],
                              ka.reshape(batch, tt, A_KV, A_DIM), va.reshape(batch, tt, A_KV, A_DIM),
                              ki.reshape(batch, tt, IDX_DIM)))
        else:
            ckv, kr, kcat, qcat = _mla_pre(x, g_attn, W["w_mla"][j], P["g_cq"][j][None, :],
                                           P["g_ckv"][j][None, :], W["w_uq"][j], W["w_uk"][j], tabs, tm)
            if past is None:
                ol = _mla_prompt(qcat, kcat, batch, tq_mla)
            else:
                pad_page = lambda a: jnp.pad(a.reshape(batch, tt, -1), ((0, 0), (0, PAGE - tt), (0, 0)))
                ol = _mla_sample(past["pt"], qcat, pad_page(kcat), past["ckv"], past["kr"], j,
                                 batch, tt, past["n_pages"], cfg["pp"])
            x = _mla_out(x, ol, W["w_uv"][j], W["w_out_mla"][j], tm)
            c_states.append((ckv.reshape(batch, tt, KV_LORA), kr.reshape(batch, tt, ROPE_D)))
        final = l == DEPTH - 1
        g_ffn = P["g_ffn"][l][None, :]
        if past is None:
            xin, groups, stride = x, batch, 1
            bufp = jnp.zeros((batch, 8, D_FF), F32)
        else:
            xin = jnp.transpose(x.reshape(batch, tt, D_MODEL), (1, 0, 2)).reshape(m, D_MODEL)
            groups, stride = 1, batch
            bufp = jnp.transpose(past["fconv"][l], (1, 0, 2)).reshape(1, (FFN_CONV - 1) * batch, D_FF)
        res = _ffn(xin, g_ffn, W["w_a"][l], W["w_g"][l], P["w_fconv"][l], P["b_fconv"][l][None, :],
                   W["w_down"][l], bufp, P["g_final"][None, :], groups, cfg["tm_ffn"], cfg["fc"], stride, final)
        xo, st = res[0], res[1][:, -1]
        if past is None:
            x = xo
            ffn_states.append(st[:, 8 - (FFN_CONV - 1):, :])
            if final:
                y_final = res[2].reshape(batch, tt, D_MODEL)
        else:
            unt = lambda a: jnp.transpose(a.reshape(tt, batch, D_MODEL), (1, 0, 2))
            x = unt(xo).reshape(m, D_MODEL)
            ffn_states.append(jnp.transpose(st.reshape(FFN_CONV - 1, batch, D_FF), (1, 0, 2)))
            if final:
                y_final = unt(res[2])
    ab = [jnp.stack(s) for s in zip(*ab_states)]
    cc = [jnp.stack(s) for s in zip(*c_states)]
    return (y_final, *ab, *cc, jnp.stack(ffn_states))


def kernel(x_prompt, x_sample, state_mlstm_C, state_mlstm_n, state_mlstm_m, state_mlstm_conv, cache_dsa_k, cache_dsa_v, cache_dsa_kidx, cache_mla_ckv, cache_mla_krope, state_ffn_conv, page_table, g_attn, g_ffn, g_final, w_in_ab, w_mconv, b_igate, b_fgate, g_mhead, w_out_ab, w_in_mla, g_cq, g_ckv, w_uq, w_uk, w_uv, w_out_mla, w_up, w_fconv, b_fconv, w_down):
    W = _prep_weights(w_in_ab, w_out_ab, w_in_mla, w_uq, w_uk, w_uv, w_out_mla, w_up, w_down)
    P = dict(g_attn=g_attn, g_ffn=g_ffn, g_final=g_final, w_mconv=w_mconv, b_igate=b_igate, b_fgate=b_fgate,
             g_mhead=g_mhead, g_cq=g_cq, g_ckv=g_ckv, w_fconv=w_fconv, b_fconv=b_fconv)
    bp, tp, _ = x_prompt.shape
    bs, ts, _ = x_sample.shape
    n_pages = page_table.shape[1]
    n_pool = cache_dsa_k.shape[1]

    cfg_p = dict(batch=bp, t=tp, tm=min(512, tp), tq_dsa=min(256, tp), tq_mla=min(256, tp),
                 tm_ffn=min(512, tp), fc=1408)
    out_p = _trunk(x_prompt.reshape(bp * tp, D_MODEL), jnp.arange(tp), P, W, None, cfg_p)

    past = dict(C=state_mlstm_C, n=state_mlstm_n, m=state_mlstm_m, mconv=state_mlstm_conv,
                k=cache_dsa_k.reshape(cache_dsa_k.shape[0], n_pool, PAGE * A_KV, A_DIM),
                v=cache_dsa_v.reshape(cache_dsa_v.shape[0], n_pool, PAGE * A_KV, A_DIM),
                kidx=jnp.swapaxes(cache_dsa_kidx, 2, 3), ckv=cache_mla_ckv,
                kr=jnp.swapaxes(cache_mla_krope, 2, 3), fconv=state_ffn_conv,
                pt=page_table.reshape(-1), n_pages=n_pages)
    ms = bs * ts
    pos_s = n_pages * PAGE + jnp.tile(jnp.arange(ts), bs)
    cfg_s = dict(batch=bs, t=ts, tm=min(512, ms), tq_dsa=None, tq_mla=None, tm_ffn=ms, fc=256,
                 pp=min(32, n_pages), pp_idx=min(64, n_pages), bg=min(32, bs))
    out_s = _trunk(x_sample.reshape(ms, D_MODEL), pos_s, P, W, past, cfg_s)
    return (out_p[0], out_s[0], *out_p[1:], *out_s[1:])
```

```python
import functools
import math

import jax
import jax.numpy as jnp
import numpy as np
from jax import lax
from jax.experimental import pallas as pl
from jax.experimental.pallas import tpu as pltpu

F32 = jnp.float32
BF16 = jnp.bfloat16
I32 = jnp.int32

D_MODEL = 1024
DEPTH = 4
PAGE = 128
M_HEADS, M_QK, M_V, M_CONV, M_CHUNK = 4, 64, 128, 4, 128
A_HEADS, A_KV, A_DIM = 4, 2, 128
IDX_HEADS, IDX_DIM, DSA_TOPK = 4, 64, 256
IDX_SCALE = (IDX_HEADS * IDX_DIM) ** -0.5
C_HEADS, Q_LORA, KV_LORA, NOPE, ROPE_D, C_V = 8, 256, 128, 128, 64, 128
MLA_SCALE = (NOPE + ROPE_D) ** -0.5
D_FF, FFN_CONV = 2816, 3
ROPE_THETA = 10000.0
EPS = 1e-6
AB_WIDTHS = (512, 512, 512, 4, 4, 512, 256, 256, 256, 64, 4)
AB_PACKED = 2944
INT_MIN = -(2 ** 31)
NEG = -1e30
LOG2E = 1.4426950408889634
VMEM_LIMIT = 56 * 1024 * 1024

_NT = (((1,), (1,)), ((), ()))


def _cp(sem):
    return pltpu.CompilerParams(dimension_semantics=sem, vmem_limit_bytes=VMEM_LIMIT)


def _dot(a, b):
    return jnp.dot(a, b, preferred_element_type=F32)


def _dot_nt(a, b):
    return lax.dot_general(a, b, _NT, preferred_element_type=F32)


def _rms(x, g):
    return x * lax.rsqrt(jnp.mean(x * x, axis=-1, keepdims=True) + EPS) * g


def _rope_piece(p, cos, sin, d):
    if d == 128:
        rot = pltpu.roll(p, 64, axis=1)
    else:
        lane = lax.broadcasted_iota(I32, p.shape, 1)
        rot = jnp.where((lane & 63) < 32, pltpu.roll(p, 96, axis=1), pltpu.roll(p, 32, axis=1))
    return p * cos + rot * sin


def _rope_wide(x, cos, sin, d):
    n = x.shape[1] // 128
    return jnp.concatenate([_rope_piece(x[:, i * 128:(i + 1) * 128], cos, sin, d) for i in range(n)], axis=1)


KEY_OF_NEG_INF = -2139095041


def _key_to_score(key):
    bits = key ^ ((key >> 31) & 0x7FFFFFFF)
    return jnp.where(key >= KEY_OF_NEG_INF, pltpu.bitcast(bits, F32), -jnp.inf)


def _lane_fold(x, op):
    out = x[:, 0:128]
    for u in range(1, x.shape[1] // 128):
        out = op(out, x[:, u * 128:(u + 1) * 128])
    return out


def _topk_threshold(count, vec_shape, k, idx_bits, key_axis):
    sl = lambda a, rs: a if rs is None else a[rs]
    t0 = jnp.where(count(lambda sc, i0, rs: sc >= 0.0) >= k, 0, INT_MIN).astype(I32)

    def bit_body(i, t):
        cand_key = t | jnp.left_shift(jnp.int32(1), 30 - i)
        cand = _key_to_score(cand_key)
        return jnp.where(count(lambda sc, i0, rs: sc >= sl(cand, rs)) >= k, cand_key, t)

    thr = _key_to_score(lax.fori_loop(0, 31, bit_body, t0))
    n_gt = count(lambda sc, i0, rs: sc > sl(thr, rs))
    n_eq = count(lambda sc, i0, rs: sc == sl(thr, rs))
    need = k - n_gt
    tie = jnp.max(jnp.where((n_eq > need) & (thr > -jnp.inf), 1, 0)) > 0

    def search():
        def idx_body(i, p):
            cand = p | jnp.left_shift(jnp.int32(1), idx_bits - 1 - i)

            def pred(kc, i0, rs):
                idx = i0 + lax.broadcasted_iota(I32, kc.shape, key_axis)
                return (kc == sl(thr, rs)) & (idx < sl(cand, rs))
            return jnp.where(count(pred) < need, cand, p)
        return lax.fori_loop(0, idx_bits, idx_body, jnp.zeros(vec_shape, I32))

    cut = lax.cond(tie, search, lambda: jnp.full(vec_shape, 2 ** 31 - 1, I32))
    return thr, cut


def _selected(sc, idx, thr, cut):
    return (sc > thr) | ((sc == thr) & (idx <= cut) & (thr > -jnp.inf))


def _flash_init(m_ref, l_ref, acc_ref):
    m_ref[...] = jnp.full(m_ref.shape, NEG, F32)
    l_ref[...] = jnp.zeros(l_ref.shape, F32)
    acc_ref[...] = jnp.zeros(acc_ref.shape, F32)


def _flash_update(s2, v, m_ref, l_ref, acc_ref, keep=None):
    if keep is not None:
        s2 = jnp.where(keep, s2, NEG)
    m_old = m_ref[...]
    m_new = jnp.maximum(m_old, jnp.max(_lane_fold(s2, jnp.maximum), axis=1, keepdims=True))
    alpha = jnp.exp2(m_old - m_new)
    n = s2.shape[1] // 128
    tiles = [jnp.exp2(s2[:, u * 128:(u + 1) * 128] - m_new) for u in range(n)]
    if keep is not None:
        tiles = [jnp.where(keep[:, u * 128:(u + 1) * 128], t, 0.0) for u, t in enumerate(tiles)]
    lsum = tiles[0]
    for t in tiles[1:]:
        lsum = lsum + t
    l_ref[...] = alpha * l_ref[...] + lsum
    p = jnp.concatenate(tiles, axis=1).astype(BF16)
    acc_ref[...] = alpha * acc_ref[...] + _dot(p, v)
    m_ref[...] = m_new


def _flash_finish(l_ref, acc_ref):
    return acc_ref[...] / jnp.sum(l_ref[...], axis=1, keepdims=True)


def _ab_proj_kernel(x_ref, g_ref, w_ref, c128_ref, s128_ref, c64_ref, s64_ref,
                    qk_ref, v_ref, o_ref, qa_ref, ka_ref, kab_ref, va_ref, vab_ref,
                    qi_ref, ki_ref, kib_ref, gt_ref):
    h = _rms(x_ref[...], g_ref[...]).astype(BF16)

    def seg(a, b):
        return _dot(h, w_ref[:, a:b])

    qk_ref[...] = seg(0, 512)
    v_ref[...] = seg(512, 1024)
    o_ref[...] = seg(1024, 1536)
    c128, s128, c64, s64 = c128_ref[...], s128_ref[...], c64_ref[...], s64_ref[...]
    qa_ref[...] = _rope_wide(seg(1536, 2048), c128, s128, 128).astype(BF16)
    ka = _rope_wide(seg(2048, 2304), c128, s128, 128)
    ka_ref[...] = ka
    kab_ref[...] = ka.astype(BF16)
    va = seg(2304, 2560)
    va_ref[...] = va
    vab_ref[...] = va.astype(BF16)
    qi_ref[...] = _rope_wide(seg(2560, 2816), c64, s64, 64).astype(BF16)
    last = seg(2816, 2944)
    ki = _rope_piece(last, c64, s64, 64)[:, :64]
    ki_ref[...] = ki
    kib_ref[...] = ki.astype(BF16)
    gt_ref[...] = last[:, 64:]


def _ab_proj(x, g, w, tabs, tm):
    m = x.shape[0]
    c128, s128, c64, s64 = tabs
    tb = c128.shape[0] // tm
    row = lambda n: pl.BlockSpec((tm, n), lambda i: (i, 0))
    tab = pl.BlockSpec((tm, 128), lambda i: (i % tb, 0))
    full = lambda a: pl.BlockSpec(a.shape, lambda i: (0,) * a.ndim)
    widths = (512, 512, 512, 512, 256, 256, 256, 256, 256, 64, 64, 64)
    dtypes = (F32, F32, F32, BF16, F32, BF16, F32, BF16, BF16, F32, BF16, F32)
    return pl.pallas_call(
        _ab_proj_kernel,
        grid=(m // tm,),
        in_specs=[row(D_MODEL), full(g), full(w), tab, tab, tab, tab],
        out_specs=[row(n) for n in widths],
        out_shape=[jax.ShapeDtypeStruct((m, n), dt) for n, dt in zip(widths, dtypes)],
        compiler_params=_cp(("parallel",)),
    )(x, g, w, c128, s128, c64, s64)


def _conv_silu_kernel(x_ref, buf_ref, w_ref, sc_ref, o_ref, work, *, tm, hb, stride, taps):
    @pl.when(pl.program_id(1) == 0)
    def _():
        work[0:hb, :] = buf_ref[0]

    x = x_ref[...]
    work[hb:hb + tm, :] = x
    y = x * w_ref[taps - 1:taps, :]
    for j in range(taps - 1):
        y = y + work[pl.ds(hb - (taps - 1 - j) * stride, tm), :] * w_ref[j:j + 1, :]
    o_ref[...] = y * jax.nn.sigmoid(y) * sc_ref[...]
    work[0:hb, :] = x[tm - hb:, :]


def _conv_silu(x, bufp, w, scale, groups, tm, stride):
    m, c = x.shape
    hb = bufp.shape[1]
    nt = m // groups // tm
    taps = w.shape[0]
    return pl.pallas_call(
        functools.partial(_conv_silu_kernel, tm=tm, hb=hb, stride=stride, taps=taps),
        grid=(groups, nt),
        in_specs=[pl.BlockSpec((tm, c), lambda g, i: (g * nt + i, 0)),
                  pl.BlockSpec((1, hb, c), lambda g, i: (g, 0, 0)),
                  pl.BlockSpec(w.shape, lambda g, i: (0, 0)),
                  pl.BlockSpec(scale.shape, lambda g, i: (0, 0))],
        out_specs=pl.BlockSpec((tm, c), lambda g, i: (g * nt + i, 0)),
        out_shape=jax.ShapeDtypeStruct((m, c), F32),
        scratch_shapes=[pltpu.VMEM((hb + tm, c), F32)],
        compiler_params=_cp(("arbitrary", "arbitrary")),
    )(x, bufp, w, scale)


def _log_sigmoid(x):
    return jnp.minimum(x, 0.0) - jnp.log1p(jnp.exp(-jnp.abs(x)))


def _mlstm_kernel(qk_ref, v_ref, o_ref, gt_ref, gtt_ref, brow_ref, bcol_ref, gh_ref,
                  c0_ref, n0_ref, m0_ref, h_ref, c1_ref, n1_ref, m1_ref, c_s, n_s, m_s, *, cl, nc, bb):
    c = pl.program_id(1)

    @pl.when(c == 0)
    def _():
        for bi in range(bb):
            for h in range(M_HEADS):
                c_s[bi * M_HEADS + h] = c0_ref[bi, h]
                n_s[bi * M_HEADS + h] = n0_ref[bi, h:h + 1, :]
                m_s[bi * M_HEADS + h] = m0_ref[bi, :, h:h + 1]

    r_i = lax.broadcasted_iota(I32, (cl, cl), 0)
    c_i = lax.broadcasted_iota(I32, (cl, cl), 1)
    causal = c_i <= r_i
    for bi in range(bb):
        _mlstm_chunk(bi, qk_ref[bi], v_ref[bi], o_ref[bi], gt_ref[bi] + brow_ref[...],
                     gtt_ref[bi] + bcol_ref[...], gh_ref, h_ref, c_s, n_s, m_s, causal, r_i, c_i, cl)

    @pl.when(c == nc - 1)
    def _():
        for bi in range(bb):
            for h in range(M_HEADS):
                c1_ref[bi, h] = c_s[bi * M_HEADS + h]
                n1_ref[bi, h:h + 1, :] = n_s[bi * M_HEADS + h]
                m1_ref[bi, :, h:h + 1] = m_s[bi * M_HEADS + h]


def _mlstm_chunk(bi, qk, vv, og, gates, gates_t, gh_ref, h_ref, c_s, n_s, m_s, causal, r_i, c_i, cl):
    li_col = gates[:, 0:4]
    lf_col = _log_sigmoid(gates[:, 4:8])
    li_row = gates_t[0:4, :]
    lf_row = _log_sigmoid(gates_t[4:8, :])
    outs = []
    for h in range(M_HEADS):
        sh = bi * M_HEADS + h
        q = qk[:, h * M_QK:(h + 1) * M_QK]
        k = qk[:, 256 + h * M_QK:256 + (h + 1) * M_QK]
        v = vv[:, h * M_V:(h + 1) * M_V]
        cm = c_s[sh]
        n = n_s[sh]
        m_prev = m_s[sh]
        b_col = jnp.sum(jnp.where(causal, lf_row[h:h + 1, :], 0.0), axis=1, keepdims=True)
        b_row = jnp.sum(jnp.where(r_i <= c_i, lf_col[:, h:h + 1], 0.0), axis=0, keepdims=True)
        dmat = jnp.where(causal, b_col - b_row + li_row[h:h + 1, :], -jnp.inf)
        inter = b_col + m_prev
        m_t = jnp.maximum(inter, jnp.max(dmat, axis=1, keepdims=True))
        iw = jnp.exp(inter - m_t)
        qb, kb, vb = q.astype(BF16), k.astype(BF16), v.astype(BF16)
        s = _dot_nt(qb, kb) * jnp.exp(dmat - m_t)
        num = iw * _dot(qb, cm.astype(BF16)) + _dot(s.astype(BF16), vb)
        den = iw * jnp.sum(q * n, axis=1, keepdims=True) + jnp.sum(s, axis=1, keepdims=True)
        hh = num / jnp.maximum(jnp.abs(den), jnp.exp(-m_t))
        m_new = m_t[cl - 1:cl, :]
        b_last = b_col[cl - 1:cl, :]
        w_end = jnp.exp(b_last - b_col + li_col[:, h:h + 1] - m_new)
        decay = jnp.exp(b_last + m_prev - m_new)
        kw = k * w_end
        c_s[sh] = decay * cm + lax.dot_general(kw.astype(BF16), vb, (((0,), (0,)), ((), ())),
                                               preferred_element_type=F32)
        n_s[sh] = decay * n + jnp.sum(kw, axis=0, keepdims=True)
        m_s[sh] = m_new
        hn = _rms(hh, gh_ref[:, h * M_V:(h + 1) * M_V])
        outs.append(hn * jax.nn.sigmoid(og[:, h * M_V:(h + 1) * M_V]))
    h_ref[bi] = jnp.concatenate(outs, axis=1)


def _mlstm(qkc, v, o, gates, gates_t, b_row, b_col, g_head, c0, n0, m0, batch, cl, bb):
    t = qkc.shape[1]
    nc = t // cl
    row = lambda n: pl.BlockSpec((bb, cl, n), lambda b, c: (b, c, 0))
    full = lambda a: pl.BlockSpec(a.shape, lambda b, c: (0,) * a.ndim)
    st_c = pl.BlockSpec((bb, M_HEADS, M_QK, M_V), lambda b, c: (b, 0, 0, 0))
    st_n = pl.BlockSpec((bb, M_HEADS, M_QK), lambda b, c: (b, 0, 0))
    st_m = pl.BlockSpec((bb, 1, M_HEADS), lambda b, c: (b, 0, 0))
    return pl.pallas_call(
        functools.partial(_mlstm_kernel, cl=cl, nc=nc, bb=bb),
        grid=(batch // bb, nc),
        in_specs=[row(512), row(512), row(512), row(64),
                  pl.BlockSpec((bb, 8, cl), lambda b, c: (b, 0, c)),
                  full(b_row), full(b_col), full(g_head), st_c, st_n, st_m],
        out_specs=[row(512), st_c, st_n, st_m],
        out_shape=[jax.ShapeDtypeStruct((batch, t, 512), F32),
                   jax.ShapeDtypeStruct((batch, M_HEADS, M_QK, M_V), F32),
                   jax.ShapeDtypeStruct((batch, M_HEADS, M_QK), F32),
                   jax.ShapeDtypeStruct((batch, 1, M_HEADS), F32)],
        scratch_shapes=[pltpu.VMEM((bb * M_HEADS, M_QK, M_V), F32),
                        pltpu.VMEM((bb * M_HEADS, 1, M_QK), F32),
                        pltpu.VMEM((bb * M_HEADS, 1, 1), F32)],
        compiler_params=_cp(("arbitrary", "arbitrary")),
    )(qkc, v, o, gates, gates_t, b_row, b_col, g_head, c0, n0, m0)


def _dsa_prompt_kernel(qa_ref, qi_ref, gtt_ref, kab_ref, vab_ref, kib_ref, o_ref,
                       keys_t, q_s, m_s, l_s, acc_s, *, tq, cw, topk, idx_bits):
    i = pl.program_id(1)
    n_ch = (i * tq + tq + cw - 1) // cw
    qpos = i * tq + lax.broadcasted_iota(I32, (1, tq), 1)
    wi = gtt_ref[0][8:12, :] * IDX_SCALE
    qi = qi_ref[...]

    def score_body(c, carry):
        off = pl.multiple_of(c * cw, cw)
        kc = kib_ref[pl.ds(off, cw), :]
        sc = jnp.zeros((cw, tq), F32)
        for h in range(IDX_HEADS):
            rel = jnp.maximum(_dot_nt(kc, qi[:, h * IDX_DIM:(h + 1) * IDX_DIM]), 0.0)
            sc = sc + rel * wi[h:h + 1, :]
        kpos = off + lax.broadcasted_iota(I32, (cw, 1), 0)
        keys_t[c] = jnp.where(kpos <= qpos, sc, -jnp.inf)
        return carry

    lax.fori_loop(0, n_ch, score_body, 0)

    def count(pred):
        def body(c, acc):
            p = pred(keys_t[c], c * cw, None).astype(F32)
            return acc + jnp.sum(p.reshape(cw // 32, 32, tq), axis=0)
        acc = lax.fori_loop(0, n_ch, body, jnp.zeros((32, tq), F32))
        return jnp.sum(acc, axis=0, keepdims=True)

    thr, cut = _topk_threshold(count, (1, tq), topk, idx_bits, key_axis=0)

    qa = qa_ref[...]
    for g in range(A_KV):
        q_s[g] = jnp.concatenate([qa[:, (2 * g) * A_DIM:(2 * g + 1) * A_DIM],
                                  qa[:, (2 * g + 1) * A_DIM:(2 * g + 2) * A_DIM]], axis=0)
        _flash_init(m_s.at[g], l_s.at[g], acc_s.at[g])

    def att_body(c, carry):
        off = pl.multiple_of(c * cw, cw)
        idx = off + lax.broadcasted_iota(I32, (cw, tq), 0)
        bias = jnp.where(_selected(keys_t[c], idx, thr, cut), 0.0, NEG).T
        bias2 = jnp.concatenate([bias, bias], axis=0)
        for g in range(A_KV):
            kg = kab_ref[pl.ds(off, cw), g * A_DIM:(g + 1) * A_DIM]
            vg = vab_ref[pl.ds(off, cw), g * A_DIM:(g + 1) * A_DIM]
            s2 = _dot_nt(q_s[g], kg) * (A_DIM ** -0.5 * LOG2E) + bias2
            _flash_update(s2, vg, m_s.at[g], l_s.at[g], acc_s.at[g])
        return carry

    lax.fori_loop(0, n_ch, att_body, 0)
    outs = []
    for g in range(A_KV):
        og = _flash_finish(l_s.at[g], acc_s.at[g])
        outs += [og[0:tq], og[tq:2 * tq]]
    o_ref[...] = jnp.concatenate(outs, axis=1)


def _dsa_prompt(qa, qi, gates_t, kab, vab, kib, batch, topk, tq):
    m = qa.shape[0]
    t = m // batch
    nq = t // tq
    cw = min(512, t)
    row = lambda n: pl.BlockSpec((tq, n), lambda b, i: (b * nq + i, 0))
    whole = lambda n: pl.BlockSpec((t, n), lambda b, i: (b, 0))
    return pl.pallas_call(
        functools.partial(_dsa_prompt_kernel, tq=tq, cw=cw, topk=topk,
                          idx_bits=max(1, (t - 1).bit_length())),
        grid=(batch, nq),
        in_specs=[row(512), row(256), pl.BlockSpec((1, 16, tq), lambda b, i: (b, 0, i)),
                  whole(256), whole(256), whole(64)],
        out_specs=row(512),
        out_shape=jax.ShapeDtypeStruct((m, 512), F32),
        scratch_shapes=[pltpu.VMEM((t // cw, cw, tq), F32),
                        pltpu.VMEM((A_KV, 2 * tq, A_DIM), BF16),
                        pltpu.VMEM((A_KV, 2 * tq, 128), F32),
                        pltpu.VMEM((A_KV, 2 * tq, 128), F32),
                        pltpu.VMEM((A_KV, 2 * tq, A_DIM), F32)],
        compiler_params=_cp(("parallel", "arbitrary")),
    )(qa, qi, gates_t, kab, vab, kib)


def _idx_scores(qi, gates, keys_mat, keys_on_lanes):
    qs = jnp.concatenate([qi[:, h * IDX_DIM:(h + 1) * IDX_DIM] for h in range(IDX_HEADS)], axis=0)
    ws = jnp.concatenate([gates[:, 8 + h:9 + h] for h in range(IDX_HEADS)], axis=0) * IDX_SCALE
    qk = _dot(qs, keys_mat) if keys_on_lanes else _dot_nt(qs, keys_mat)
    rel = jnp.maximum(qk, 0.0) * ws
    t = qi.shape[0]
    sc = rel[0:t]
    for h in range(1, IDX_HEADS):
        sc = sc + rel[h * t:(h + 1) * t]
    return sc


def _dsa_scores_sample_kernel(pt_ref, qi_ref, gt_ref, knew_ref, *rest, pp):
    pages = rest[:pp]
    kp_ref, kn_ref = rest[pp], rest[pp + 1]
    qi = qi_ref[...]
    gates = gt_ref[...]
    kcat = jnp.concatenate([p[...] for p in pages], axis=1).astype(BF16)
    kp_ref[0] = _idx_scores(qi, gates, kcat, True)
    sn = _idx_scores(qi, gates, knew_ref[0], False)
    t = qi.shape[0]
    vis = lax.broadcasted_iota(I32, (t, PAGE), 1) <= lax.broadcasted_iota(I32, (t, PAGE), 0)
    kn_ref[0] = jnp.where(vis, sn, -jnp.inf)


def _page_specs(shape_tail, layer, n_pages, pp):
    nd = len(shape_tail)

    def spec(u):
        return pl.BlockSpec((None, None) + shape_tail,
                            lambda b, j, pt: (layer, pt[b * n_pages + j * pp + u]) + (0,) * nd)
    return [spec(u) for u in range(pp)]


def _dsa_scores_sample(pt, qi, gates, knew, pool, layer, batch, tt, n_pages, pp):
    grid_spec = pltpu.PrefetchScalarGridSpec(
        num_scalar_prefetch=1,
        grid=(batch, n_pages // pp),
        in_specs=[pl.BlockSpec((tt, 256), lambda b, j, pt: (b, 0)),
                  pl.BlockSpec((tt, 64), lambda b, j, pt: (b, 0)),
                  pl.BlockSpec((1, PAGE, IDX_DIM), lambda b, j, pt: (b, 0, 0))]
                 + _page_specs((IDX_DIM, PAGE), layer, n_pages, pp),
        out_specs=[pl.BlockSpec((1, tt, pp * PAGE), lambda b, j, pt: (b, 0, j)),
                   pl.BlockSpec((1, tt, PAGE), lambda b, j, pt: (b, 0, 0))],
    )
    return pl.pallas_call(
        functools.partial(_dsa_scores_sample_kernel, pp=pp),
        grid_spec=grid_spec,
        out_shape=[jax.ShapeDtypeStruct((batch, tt, n_pages * PAGE), F32),
                   jax.ShapeDtypeStruct((batch, tt, PAGE), F32)],
        compiler_params=_cp(("parallel", "arbitrary")),
    )(pt, qi, gates, knew, *([pool] * pp))


def _thresh_sample_kernel(kp_ref, kn_ref, thr_ref, cut_ref, *, rows, cw, n_past, topk, idx_bits):
    def count(pred):
        tt = kp_ref.shape[1]
        rb = min(128, rows)
        parts = []
        for r0 in range(0, rows, rb):
            rs = slice(r0, r0 + rb)
            bs = slice(r0 // tt, (r0 + rb) // tt)
            acc = pred(kn_ref[bs].reshape(rb, PAGE), n_past * cw, rs).astype(F32)
            for c in range(n_past):
                kc = kp_ref[bs, :, c * cw:(c + 1) * cw].reshape(rb, cw)
                acc = acc + _lane_fold(pred(kc, c * cw, rs).astype(F32), jnp.add)
            parts.append(acc)
        return jnp.sum(jnp.concatenate(parts, axis=0), axis=1, keepdims=True)

    thr, cut = _topk_threshold(count, (rows, 1), topk, idx_bits, key_axis=1)
    thr_ref[...] = jnp.broadcast_to(thr, (rows, PAGE)).reshape(thr_ref.shape)
    cut_ref[...] = jnp.broadcast_to(cut, (rows, PAGE)).reshape(cut_ref.shape)


def _thresh_sample(keys_past, keys_new, topk, bg):
    batch, tt, lp = keys_past.shape
    cw = 512
    rows = bg * tt
    n_past = lp // cw
    blk = lambda n: pl.BlockSpec((bg, tt, n), lambda i: (i, 0, 0))
    return pl.pallas_call(
        functools.partial(_thresh_sample_kernel, rows=rows, cw=cw, n_past=n_past, topk=topk,
                          idx_bits=(lp + PAGE - 1).bit_length()),
        grid=(batch // bg,),
        in_specs=[blk(lp), blk(PAGE)],
        out_specs=[blk(PAGE), blk(PAGE)],
        out_shape=[jax.ShapeDtypeStruct((batch, tt, PAGE), F32), jax.ShapeDtypeStruct((batch, tt, PAGE), I32)],
        compiler_params=_cp(("parallel",)),
    )(keys_past, keys_new)


def _dsa_attend_sample_kernel(pt_ref, qa_ref, thr_ref, cut_ref, kp_ref, kn_ref, knew_ref, vnew_ref, *rest,
                              pp, n_steps, n_past_keys):
    kpages, vpages = rest[:pp], rest[pp:2 * pp]
    o_ref = rest[2 * pp]
    m_s, l_s, acc_s = rest[2 * pp + 1:]
    j = pl.program_id(1)
    tt = qa_ref.shape[0]

    @pl.when(j == 0)
    def _():
        for g in range(A_KV):
            _flash_init(m_s.at[g], l_s.at[g], acc_s.at[g])

    qa = qa_ref[...]
    qg = [jnp.concatenate([qa[:, (2 * g) * A_DIM:(2 * g + 1) * A_DIM],
                           qa[:, (2 * g + 1) * A_DIM:(2 * g + 2) * A_DIM]], axis=0) for g in range(A_KV)]
    thr = thr_ref[0][:, 0:1]
    cut = cut_ref[0][:, 0:1]

    def attend(kc, first_idx, kv_of_group):
        idx = first_idx + lax.broadcasted_iota(I32, kc.shape, 1)
        bias = jnp.where(_selected(kc, idx, thr, cut), 0.0, NEG)
        bias2 = jnp.concatenate([bias, bias], axis=0)
        for g in range(A_KV):
            kg, vg = kv_of_group(g)
            s2 = _dot_nt(qg[g], kg) * (A_DIM ** -0.5 * LOG2E) + bias2
            _flash_update(s2, vg, m_s.at[g], l_s.at[g], acc_s.at[g])

    def paged(g):
        rows = lambda p: p[pl.ds(g, PAGE, stride=A_KV), :]
        return (jnp.concatenate([rows(p) for p in kpages], axis=0).astype(BF16),
                jnp.concatenate([rows(p) for p in vpages], axis=0).astype(BF16))

    attend(kp_ref[0], j * (pp * PAGE), paged)

    @pl.when(j == n_steps - 1)
    def _():
        attend(kn_ref[0], n_past_keys,
               lambda g: (knew_ref[0][:, g * A_DIM:(g + 1) * A_DIM], vnew_ref[0][:, g * A_DIM:(g + 1) * A_DIM]))
        outs = []
        for g in range(A_KV):
            og = _flash_finish(l_s.at[g], acc_s.at[g])
            outs += [og[0:tt], og[tt:2 * tt]]
        o_ref[...] = jnp.concatenate(outs, axis=1)


def _dsa_attend_sample(pt, qa, thr, cut, keys_past, keys_new, knew, vnew, pool_k, pool_v,
                       layer, batch, tt, n_pages, pp):
    n_steps = n_pages // pp
    kvw = A_KV * A_DIM
    grid_spec = pltpu.PrefetchScalarGridSpec(
        num_scalar_prefetch=1,
        grid=(batch, n_steps),
        in_specs=[pl.BlockSpec((tt, 512), lambda b, j, pt: (b, 0)),
                  pl.BlockSpec((1, tt, PAGE), lambda b, j, pt: (b, 0, 0)),
                  pl.BlockSpec((1, tt, PAGE), lambda b, j, pt: (b, 0, 0)),
                  pl.BlockSpec((1, tt, pp * PAGE), lambda b, j, pt: (b, 0, j)),
                  pl.BlockSpec((1, tt, PAGE), lambda b, j, pt: (b, 0, 0)),
                  pl.BlockSpec((1, PAGE, kvw), lambda b, j, pt: (b, 0, 0)),
                  pl.BlockSpec((1, PAGE, kvw), lambda b, j, pt: (b, 0, 0))]
                 + _page_specs((PAGE * A_KV, A_DIM), layer, n_pages, pp)
                 + _page_specs((PAGE * A_KV, A_DIM), layer, n_pages, pp),
        out_specs=pl.BlockSpec((tt, 512), lambda b, j, pt: (b, 0)),
        scratch_shapes=[pltpu.VMEM((A_KV, 2 * tt, 128), F32),
                        pltpu.VMEM((A_KV, 2 * tt, 128), F32),
                        pltpu.VMEM((A_KV, 2 * tt, A_DIM), F32)],
    )
    return pl.pallas_call(
        functools.partial(_dsa_attend_sample_kernel, pp=pp, n_steps=n_steps, n_past_keys=n_pages * PAGE),
        grid_spec=grid_spec,
        out_shape=jax.ShapeDtypeStruct((batch * tt, 512), F32),
        compiler_params=_cp(("parallel", "arbitrary")),
    )(pt, qa, thr, cut, keys_past, keys_new, knew, vnew, *([pool_k] * pp), *([pool_v] * pp))


def _mla_pre_kernel(x_ref, g_ref, win_ref, gq_ref, gkv_ref, wuq_ref, wuk_ref, c64_ref, s64_ref,
                    ckv_ref, kr_ref, kcat_ref, qcat_ref):
    h = _rms(x_ref[...], g_ref[...]).astype(BF16)
    cq = _rms(_dot(h, win_ref[:, 0:Q_LORA]), gq_ref[...])
    rest = _dot(h, win_ref[:, Q_LORA:Q_LORA + 256])
    ckv = _rms(rest[:, 0:KV_LORA], gkv_ref[...])
    ckv_ref[...] = ckv
    c64, s64 = c64_ref[...], s64_ref[...]
    lane = lax.broadcasted_iota(I32, (h.shape[0], 128), 1)
    kr = jnp.where(lane < ROPE_D, _rope_piece(rest[:, 128:256], c64, s64, 64), 0.0)
    kr_ref[...] = kr[:, :ROPE_D]
    kcat_ref[...] = jnp.concatenate([ckv, kr], axis=1).astype(BF16)
    q = _dot(cq.astype(BF16), wuq_ref[...])
    qr = _rope_wide(q[:, 1024:1536], c64, s64, 64)
    pieces = []
    for hh in range(C_HEADS):
        pieces.append(_dot(q[:, hh * NOPE:(hh + 1) * NOPE].astype(BF16), wuk_ref[hh]))
        pair = qr[:, (hh // 2) * 128:(hh // 2 + 1) * 128]
        if hh % 2:
            pair = pltpu.roll(pair, ROPE_D, axis=1)
        pieces.append(jnp.where(lane < ROPE_D, pair, 0.0))
    qcat_ref[...] = jnp.concatenate(pieces, axis=1).astype(BF16)


def _mla_pre(x, g, win, gq, gkv, wuq, wuk, tabs, tm):
    m = x.shape[0]
    c64, s64 = tabs[2], tabs[3]
    tb = c64.shape[0] // tm
    row = lambda n: pl.BlockSpec((tm, n), lambda i: (i, 0))
    tab = pl.BlockSpec((tm, 128), lambda i: (i % tb, 0))
    full = lambda a: pl.BlockSpec(a.shape, lambda i: (0,) * a.ndim)
    widths = (KV_LORA, ROPE_D, 256, C_HEADS * 256)
    dtypes = (F32, F32, BF16, BF16)
    return pl.pallas_call(
        _mla_pre_kernel,
        grid=(m // tm,),
        in_specs=[row(D_MODEL), full(g), full(win), full(gq), full(gkv), full(wuq), full(wuk), tab, tab],
        out_specs=[row(n) for n in widths],
        out_shape=[jax.ShapeDtypeStruct((m, n), dt) for n, dt in zip(widths, dtypes)],
        compiler_params=_cp(("parallel",)),
    )(x, g, win, gq, gkv, wuq, wuk, c64, s64)


def _stack_heads(qcat):
    return jnp.concatenate([qcat[:, h * 256:(h + 1) * 256] for h in range(C_HEADS)], axis=0)


def _unstack_heads(o, t):
    return jnp.concatenate([o[h * t:(h + 1) * t] for h in range(C_HEADS)], axis=1)


def _mla_prompt_kernel(q_ref, k_ref, o_ref, q_s, m_s, l_s, acc_s, *, tq):
    i = pl.program_id(1)
    q_s[...] = _stack_heads(q_ref[...])
    _flash_init(m_s, l_s, acc_s)

    def step(c, keep):
        kc = k_ref[pl.ds(pl.multiple_of(c * tq, tq), tq), :]
        s2 = _dot_nt(q_s[...], kc) * (MLA_SCALE * LOG2E)
        _flash_update(s2, kc[:, 0:KV_LORA], m_s, l_s, acc_s, keep=keep)

    def body(c, carry):
        step(c, None)
        return carry

    lax.fori_loop(0, i, body, 0)
    rows = C_HEADS * tq
    t_in = lax.broadcasted_iota(I32, (rows, tq), 0) & (tq - 1)
    step(i, lax.broadcasted_iota(I32, (rows, tq), 1) <= t_in)
    o_ref[...] = _unstack_heads(_flash_finish(l_s, acc_s), tq).astype(BF16)


def _mla_prompt(qcat, kcat, batch, tq):
    m = qcat.shape[0]
    t = m // batch
    nq = t // tq
    return pl.pallas_call(
        functools.partial(_mla_prompt_kernel, tq=tq),
        grid=(batch, nq),
        in_specs=[pl.BlockSpec((tq, C_HEADS * 256), lambda b, i: (b * nq + i, 0)),
                  pl.BlockSpec((t, 256), lambda b, i: (b, 0))],
        out_specs=pl.BlockSpec((tq, C_HEADS * KV_LORA), lambda b, i: (b * nq + i, 0)),
        out_shape=jax.ShapeDtypeStruct((m, C_HEADS * KV_LORA), BF16),
        scratch_shapes=[pltpu.VMEM((C_HEADS * tq, 256), BF16),
                        pltpu.VMEM((C_HEADS * tq, 128), F32),
                        pltpu.VMEM((C_HEADS * tq, 128), F32),
                        pltpu.VMEM((C_HEADS * tq, KV_LORA), F32)],
        compiler_params=_cp(("parallel", "arbitrary")),
    )(qcat, kcat)


def _mla_sample_kernel(pt_ref, q_ref, knew_ref, *rest, pp, n_steps):
    cpages, rpages = rest[:pp], rest[pp:2 * pp]
    o_ref = rest[2 * pp]
    m_s, l_s, acc_s = rest[2 * pp + 1:]
    j = pl.program_id(1)
    tt = q_ref.shape[0]

    @pl.when(j == 0)
    def _():
        _flash_init(m_s, l_s, acc_s)

    qs = _stack_heads(q_ref[...])
    ccat = jnp.concatenate([p[...] for p in cpages], axis=0).astype(BF16)
    rcat = jnp.concatenate([p[...] for p in rpages], axis=1).astype(BF16)
    s2 = (_dot_nt(qs[:, 0:KV_LORA], ccat) + _dot(qs[:, KV_LORA:KV_LORA + ROPE_D], rcat)) * (MLA_SCALE * LOG2E)
    _flash_update(s2, ccat, m_s, l_s, acc_s)

    @pl.when(j == n_steps - 1)
    def _():
        kn = knew_ref[0]
        rows = C_HEADS * tt
        t_in = lax.broadcasted_iota(I32, (rows, PAGE), 0) & (tt - 1)
        keep = lax.broadcasted_iota(I32, (rows, PAGE), 1) <= t_in
        _flash_update(_dot_nt(qs, kn) * (MLA_SCALE * LOG2E), kn[:, 0:KV_LORA], m_s, l_s, acc_s, keep=keep)
        o_ref[...] = _unstack_heads(_flash_finish(l_s, acc_s), tt).astype(BF16)


def _mla_sample(pt, qcat, knew, pool_c, pool_r, layer, batch, tt, n_pages, pp):
    n_steps = n_pages // pp
    grid_spec = pltpu.PrefetchScalarGridSpec(
        num_scalar_prefetch=1,
        grid=(batch, n_steps),
        in_specs=[pl.BlockSpec((tt, C_HEADS * 256), lambda b, j, pt: (b, 0)),
                  pl.BlockSpec((1, PAGE, 256), lambda b, j, pt: (b, 0, 0))]
                 + _page_specs((PAGE, KV_LORA), layer, n_pages, pp)
                 + _page_specs((ROPE_D, PAGE), layer, n_pages, pp),
        out_specs=pl.BlockSpec((tt, C_HEADS * KV_LORA), lambda b, j, pt: (b, 0)),
        scratch_shapes=[pltpu.VMEM((C_HEADS * tt, 128), F32),
                        pltpu.VMEM((C_HEADS * tt, 128), F32),
                        pltpu.VMEM((C_HEADS * tt, KV_LORA), F32)],
    )
    return pl.pallas_call(
        functools.partial(_mla_sample_kernel, pp=pp, n_steps=n_steps),
        grid_spec=grid_spec,
        out_shape=jax.ShapeDtypeStruct((batch * tt, C_HEADS * KV_LORA), BF16),
        compiler_params=_cp(("parallel", "arbitrary")),
    )(pt, qcat, knew, *([pool_c] * pp), *([pool_r] * pp))


def _ab_out_kernel(x_ref, hm_ref, oa_ref, w_ref, o_ref):
    o_ref[...] = (x_ref[...] + _dot(hm_ref[...].astype(BF16), w_ref[0:512, :])
                  + _dot(oa_ref[...].astype(BF16), w_ref[512:1024, :]))


def _ab_out(x, hm, oa, w, tm):
    m = x.shape[0]
    row = lambda n: pl.BlockSpec((tm, n), lambda i: (i, 0))
    return pl.pallas_call(
        _ab_out_kernel,
        grid=(m // tm,),
        in_specs=[row(D_MODEL), row(512), row(512), pl.BlockSpec(w.shape, lambda i: (0, 0))],
        out_specs=row(D_MODEL),
        out_shape=jax.ShapeDtypeStruct((m, D_MODEL), F32),
        compiler_params=_cp(("parallel",)),
    )(x, hm, oa, w)


def _mla_out_kernel(x_ref, ol_ref, wuv_ref, w_ref, o_ref):
    ol = ol_ref[...]
    o = jnp.concatenate([_dot(ol[:, h * KV_LORA:(h + 1) * KV_LORA], wuv_ref[h]) for h in range(C_HEADS)],
                        axis=1)
    o_ref[...] = x_ref[...] + _dot(o.astype(BF16), w_ref[...])


def _mla_out(x, ol, wuv, w, tm):
    m = x.shape[0]
    row = lambda n: pl.BlockSpec((tm, n), lambda i: (i, 0))
    return pl.pallas_call(
        _mla_out_kernel,
        grid=(m // tm,),
        in_specs=[row(D_MODEL), row(1024), pl.BlockSpec(wuv.shape, lambda i: (0, 0, 0)),
                  pl.BlockSpec(w.shape, lambda i: (0, 0))],
        out_specs=row(D_MODEL),
        out_shape=jax.ShapeDtypeStruct((m, D_MODEL), F32),
        compiler_params=_cp(("parallel",)),
    )(x, ol, wuv, w)


def _load_tm(ref, lead, steps):
    if steps <= 1:
        return ref[lead + (slice(None), slice(None))]
    return jnp.concatenate([ref[:, s, :] for s in range(steps)], axis=0)


def _store_tm(ref, lead, val, steps):
    if steps <= 1:
        ref[lead + (slice(None), slice(None))] = val
        return
    nb = ref.shape[0]
    for s in range(steps):
        ref[:, s, :] = val[s * nb:(s + 1) * nb]


def _ffn_kernel(x_ref, g_ref, wa_ref, wg_ref, wc_ref, bc_ref, wd_ref, buf_ref, gf_ref,
                o_ref, st_ref, y_ref, h_s, acc_s, halo, work, *, tm, hb, stride, nj, final, xsteps, bsteps):
    i = pl.program_id(1)
    j = pl.program_id(2)

    @pl.when(j == 0)
    def _():
        h_s[...] = _rms(_load_tm(x_ref, (), xsteps), g_ref[...]).astype(BF16)
        acc_s[...] = jnp.zeros(acc_s.shape, F32)

    @pl.when(i == 0)
    def _():
        halo[j] = _load_tm(buf_ref, (0,), bsteps)

    h = h_s[...]
    a = _dot(h, wa_ref[...])
    gg = _dot(h, wg_ref[...])
    work[0:hb, :] = halo[j]
    work[hb:hb + tm, :] = gg
    gc = gg * wc_ref[FFN_CONV - 1:FFN_CONV, :]
    for t in range(FFN_CONV - 1):
        gc = gc + work[pl.ds(hb - (FFN_CONV - 1 - t) * stride, tm), :] * wc_ref[t:t + 1, :]
    last = gg[tm - hb:, :]
    halo[j] = last
    _store_tm(st_ref, (0, 0), last, bsteps)
    gc = gc + bc_ref[...]
    p = a * (gc * jax.nn.sigmoid(gc))
    acc_s[...] += _dot(p.astype(BF16), wd_ref[...])

    @pl.when(j == nj - 1)
    def _():
        xn = _load_tm(x_ref, (), xsteps) + acc_s[...]
        _store_tm(o_ref, (), xn, xsteps)
        if final:
            _store_tm(y_ref, (), _rms(xn, gf_ref[...]), xsteps)


def _ffn(x, g, wa, wg, wc, bc, wd, bufp, g_final, groups, tm, fc, stride, final, xsteps=1, bsteps=1):
    nj = D_FF // fc
    vec = lambda a: pl.BlockSpec(a.shape, lambda gi, i, j: (0, 0))
    if xsteps > 1:
        nb = x.shape[0]
        assert groups == 1 and tm == nb * xsteps
        nt, hb = 1, nb * bsteps
        row = pl.BlockSpec((nb, xsteps, D_MODEL), lambda gi, i, j: (0, 0, 0))
        st = st_out = pl.BlockSpec((nb, bsteps, fc), lambda gi, i, j: (0, 0, j))
        st_shape = (nb, bsteps, D_FF)
    else:
        hb = bufp.shape[1]
        nt = x.shape[0] // groups // tm
        row = pl.BlockSpec((tm, D_MODEL), lambda gi, i, j: (gi * nt + i, 0))
        st = pl.BlockSpec((1, hb, fc), lambda gi, i, j: (gi, 0, j))
        st_out = pl.BlockSpec((1, 1, hb, fc), lambda gi, i, j: (gi, i, 0, j))
        st_shape = (groups, nt, hb, D_FF)
    outs = [row, st_out] + ([row] if final else [])
    shapes = [jax.ShapeDtypeStruct(x.shape, F32), jax.ShapeDtypeStruct(st_shape, F32)]
    if final:
        shapes.append(jax.ShapeDtypeStruct(x.shape, F32))

    static = dict(tm=tm, hb=hb, stride=stride, nj=nj, xsteps=xsteps, bsteps=bsteps)

    def kern(*refs):
        if final:
            return _ffn_kernel(*refs, final=True, **static)
        ins, rest = refs[:9], refs[9:]
        return _ffn_kernel(*ins, rest[0], rest[1], None, *rest[2:], final=False, **static)

    return pl.pallas_call(
        kern,
        grid=(groups, nt, nj),
        in_specs=[row, vec(g),
                  pl.BlockSpec((D_MODEL, fc), lambda gi, i, j: (0, j)),
                  pl.BlockSpec((D_MODEL, fc), lambda gi, i, j: (0, j)),
                  pl.BlockSpec((FFN_CONV, fc), lambda gi, i, j: (0, j)),
                  pl.BlockSpec((1, fc), lambda gi, i, j: (0, j)),
                  pl.BlockSpec((fc, D_MODEL), lambda gi, i, j: (j, 0)),
                  st, vec(g_final)],
        out_specs=outs,
        out_shape=shapes,
        scratch_shapes=[pltpu.VMEM((tm, D_MODEL), BF16),
                        pltpu.VMEM((tm, D_MODEL), F32),
                        pltpu.VMEM((nj, hb, fc), F32),
                        pltpu.VMEM((hb + tm, fc), F32)],
        compiler_params=_cp(("arbitrary", "arbitrary", "arbitrary")),
    )(x, g, wa, wg, wc, bc, wd, bufp, g_final)


def _rope_tables(pos, d):
    half = d // 2
    inv = ROPE_THETA ** (-jnp.arange(half, dtype=F32) * (2.0 / d))
    ang = pos.astype(F32)[:, None] * inv[None, :]
    cos, sin = jnp.cos(ang), jnp.sin(ang)
    reps = 128 // d
    return (jnp.tile(jnp.concatenate([cos, cos], axis=1), (1, reps)),
            jnp.tile(jnp.concatenate([-sin, sin], axis=1), (1, reps)))


def _prep_weights(w_in_ab, w_out_ab, w_in_mla, w_uq, w_uk, w_uv, w_out_mla, w_up, w_down):
    cuts = np.cumsum((0,) + AB_WIDTHS)
    seg = lambda w, k: w[:, :, cuts[k]:cuts[k + 1]]
    n_ab = w_in_ab.shape[0]
    pad = jnp.zeros((n_ab, D_MODEL, 52), F32)
    w_ab = jnp.concatenate([seg(w_in_ab, 0), seg(w_in_ab, 1), seg(w_in_ab, 2), seg(w_in_ab, 5),
                            seg(w_in_ab, 6), seg(w_in_ab, 7), seg(w_in_ab, 8), seg(w_in_ab, 9),
                            seg(w_in_ab, 3), seg(w_in_ab, 4), seg(w_in_ab, 10), pad], axis=2).astype(BF16)
    n_c = w_in_mla.shape[0]
    w_mla = jnp.concatenate([w_in_mla, jnp.zeros((n_c, D_MODEL, 64), F32)], axis=2).astype(BF16)
    uq = w_uq.reshape(n_c, Q_LORA, C_HEADS, NOPE + ROPE_D)
    w_uq2 = jnp.concatenate([uq[..., :NOPE].reshape(n_c, Q_LORA, C_HEADS * NOPE),
                             uq[..., NOPE:].reshape(n_c, Q_LORA, C_HEADS * ROPE_D)], axis=2).astype(BF16)
    return dict(w_ab=w_ab, w_out_ab=w_out_ab.astype(BF16), w_mla=w_mla, w_uq=w_uq2,
                w_uk=w_uk.astype(BF16), w_uv=w_uv.astype(BF16), w_out_mla=w_out_mla.astype(BF16),
                w_a=w_up[:, :, :D_FF].astype(BF16), w_g=w_up[:, :, D_FF:].astype(BF16),
                w_down=w_down.astype(BF16))


def _front_pad(buf, hb):
    g, r, c = buf.shape
    return jnp.concatenate([jnp.zeros((g, hb - r, c), F32), buf], axis=1)


def _trunk(x, pos_rows, P, W, past, cfg):
    batch, tt = cfg["batch"], cfg["t"]
    tm, tq_dsa, tq_mla = cfg["tm"], cfg["tq_dsa"], cfg["tq_mla"]
    m = batch * tt
    c128, s128 = _rope_tables(pos_rows, 128)
    c64, s64 = _rope_tables(pos_rows, 64)
    tabs = (c128, s128, c64, s64)
    kscale = jnp.concatenate([jnp.ones((1, 256), F32), jnp.full((1, 256), M_QK ** -0.5, F32)], axis=1)
    ab_states, c_states, ffn_states = [], [], []
    y_final = None
    for l in range(DEPTH):
        j = l // 2
        g_attn = P["g_attn"][l][None, :]
        if l % 2 == 0:
            (qk_m, v_m, o_m, qa, ka, kab, va, vab, qi, ki, kib, gates) = _ab_proj(x, g_attn, W["w_ab"][j], tabs, tm)
            if past is None:
                conv_buf = jnp.zeros((batch, M_CONV - 1, 512), F32)
                c0 = jnp.zeros((batch, M_HEADS, M_QK, M_V), F32)
                n0 = jnp.zeros((batch, M_HEADS, M_QK), F32)
                m0 = jnp.zeros((batch, 1, M_HEADS), F32)
            else:
                conv_buf = past["mconv"][j]
                c0, n0, m0 = past["C"][j], past["n"][j], past["m"][j][:, None, :]
            qkc = _conv_silu(qk_m, _front_pad(conv_buf, 8), P["w_mconv"][j], kscale,
                             groups=batch, tm=min(tm, tt), stride=1)
            cl = math.gcd(tt, M_CHUNK)
            gates_t = jnp.transpose(gates[:, :16].reshape(batch, tt, 16), (0, 2, 1))
            bias = jnp.concatenate([P["b_igate"][j], P["b_fgate"][j]])
            b_row = jnp.concatenate([bias, jnp.zeros((56,), F32)])[None, :]
            b3 = lambda a: a.reshape(batch, tt, a.shape[-1])
            h_m, c1, n1, m1 = _mlstm(b3(qkc), b3(v_m), b3(o_m), b3(gates), gates_t, b_row, bias[:, None],
                                     P["g_mhead"][j].reshape(1, 512), c0, n0, m0, batch, cl, cfg["bb"])
            h_m = h_m.reshape(m, 512)
            if past is None:
                o_a = _dsa_prompt(qa, qi, gates_t, kab, vab, kib, batch, min(DSA_TOPK, tt // 4), tq_dsa)
            else:
                pt, n_pages, pp = past["pt"], past["n_pages"], cfg["pp"]
                pad_page = lambda a: jnp.pad(a.reshape(batch, tt, -1), ((0, 0), (0, PAGE - tt), (0, 0)))
                keys_past, keys_new = _dsa_scores_sample(pt, qi, gates, pad_page(kib), past["kidx"], j,
                                                         batch, tt, n_pages, cfg["pp_idx"])
                topk = min(DSA_TOPK, (n_pages * PAGE + tt) // 4)
                thr, cut = _thresh_sample(keys_past, keys_new, topk, cfg["bg"])
                o_a = _dsa_attend_sample(pt, qa, thr, cut, keys_past, keys_new, pad_page(kab), pad_page(vab),
                                         past["k"], past["v"], j, batch, tt, n_pages, pp)
            x = _ab_out(x, h_m, o_a, W["w_out_ab"][j], tm)
            ab_states.append((c1, n1, m1.reshape(batch, M_HEADS),
                              qk_m.reshape(batch, tt, 512)[:, tt - (M_CONV - 1):],
                              ka.reshape(batch, tt, A_KV, A_DIM), va.reshape(batch, tt, A_KV, A_DIM),
                              ki.reshape(batch, tt, IDX_DIM)))
        else:
            ckv, kr, kcat, qcat = _mla_pre(x, g_attn, W["w_mla"][j], P["g_cq"][j][None, :],
                                           P["g_ckv"][j][None, :], W["w_uq"][j], W["w_uk"][j], tabs, tm)
            if past is None:
                ol = _mla_prompt(qcat, kcat, batch, tq_mla)
            else:
                pad_page = lambda a: jnp.pad(a.reshape(batch, tt, -1), ((0, 0), (0, PAGE - tt), (0, 0)))
                ol = _mla_sample(past["pt"], qcat, pad_page(kcat), past["ckv"], past["kr"], j,
                                 batch, tt, past["n_pages"], cfg["pp"])
            x = _mla_out(x, ol, W["w_uv"][j], W["w_out_mla"][j], tm)
            c_states.append((ckv.reshape(batch, tt, KV_LORA), kr.reshape(batch, tt, ROPE_D)))
        final = l == DEPTH - 1
        g_ffn = P["g_ffn"][l][None, :]
        ffn_w = (g_ffn, W["w_a"][l], W["w_g"][l], P["w_fconv"][l], P["b_fconv"][l][None, :], W["w_down"][l])
        if past is None:
            res = _ffn(x, *ffn_w, jnp.zeros((batch, 8, D_FF), F32), P["g_final"][None, :], batch,
                       cfg["tm_ffn"], cfg["fc"], 1, final)
            ffn_states.append(res[1][:, -1, 8 - (FFN_CONV - 1):, :])
        else:
            res = _ffn(x.reshape(batch, tt, D_MODEL), *ffn_w, past["fconv"][l], P["g_final"][None, :], 1,
                       cfg["tm_ffn"], cfg["fc"], batch, final, xsteps=tt, bsteps=FFN_CONV - 1)
            ffn_states.append(res[1])
        x = res[0].reshape(m, D_MODEL)
        if final:
            y_final = res[2].reshape(batch, tt, D_MODEL)
    ab = [jnp.stack(s) for s in zip(*ab_states)]
    cc = [jnp.stack(s) for s in zip(*c_states)]
    return (y_final, *ab, *cc, jnp.stack(ffn_states))


def kernel(x_prompt, x_sample, state_mlstm_C, state_mlstm_n, state_mlstm_m, state_mlstm_conv, cache_dsa_k, cache_dsa_v, cache_dsa_kidx, cache_mla_ckv, cache_mla_krope, state_ffn_conv, page_table, g_attn, g_ffn, g_final, w_in_ab, w_mconv, b_igate, b_fgate, g_mhead, w_out_ab, w_in_mla, g_cq, g_ckv, w_uq, w_uk, w_uv, w_out_mla, w_up, w_fconv, b_fconv, w_down):
    W = _prep_weights(w_in_ab, w_out_ab, w_in_mla, w_uq, w_uk, w_uv, w_out_mla, w_up, w_down)
    P = dict(g_attn=g_attn, g_ffn=g_ffn, g_final=g_final, w_mconv=w_mconv, b_igate=b_igate, b_fgate=b_fgate,
             g_mhead=g_mhead, g_cq=g_cq, g_ckv=g_ckv, w_fconv=w_fconv, b_fconv=b_fconv)
    bp, tp, _ = x_prompt.shape
    bs, ts, _ = x_sample.shape
    n_pages = page_table.shape[1]
    n_pool = cache_dsa_k.shape[1]

    cfg_p = dict(batch=bp, t=tp, tm=min(512, tp), tq_dsa=min(256, tp), tq_mla=min(256, tp),
                 tm_ffn=min(512, tp), fc=1408, bb=math.gcd(bp, 4))
    out_p = _trunk(x_prompt.reshape(bp * tp, D_MODEL), jnp.arange(tp), P, W, None, cfg_p)

    past = dict(C=state_mlstm_C, n=state_mlstm_n, m=state_mlstm_m, mconv=state_mlstm_conv,
                k=cache_dsa_k.reshape(cache_dsa_k.shape[0], n_pool, PAGE * A_KV, A_DIM),
                v=cache_dsa_v.reshape(cache_dsa_v.shape[0], n_pool, PAGE * A_KV, A_DIM),
                kidx=jnp.swapaxes(cache_dsa_kidx, 2, 3), ckv=cache_mla_ckv,
                kr=jnp.swapaxes(cache_mla_krope, 2, 3), fconv=state_ffn_conv,
                pt=page_table.reshape(-1), n_pages=n_pages)
    ms = bs * ts
    pos_s = n_pages * PAGE + jnp.tile(jnp.arange(ts), bs)
    cfg_s = dict(batch=bs, t=ts, tm=min(512, ms), tq_dsa=None, tq_mla=None, tm_ffn=ms, fc=256,
                 pp=min(32, n_pages), pp_idx=min(64, n_pages), bg=min(32, bs), bb=math.gcd(bs, 8))
    out_s = _trunk(x_sample.reshape(ms, D_MODEL), pos_s, P, W, past, cfg_s)
    return (out_p[0], out_s[0], *out_p[1:], *out_s[1:])
```

```python
import functools
import math

import jax
import jax.numpy as jnp
import numpy as np
from jax import lax
from jax.experimental import pallas as pl
from jax.experimental.pallas import tpu as pltpu

F32 = jnp.float32
BF16 = jnp.bfloat16
I32 = jnp.int32

D_MODEL = 1024
DEPTH = 4
PAGE = 128
M_HEADS, M_QK, M_V, M_CONV, M_CHUNK = 4, 64, 128, 4, 128
A_HEADS, A_KV, A_DIM = 4, 2, 128
IDX_HEADS, IDX_DIM, DSA_TOPK = 4, 64, 256
IDX_SCALE = (IDX_HEADS * IDX_DIM) ** -0.5
C_HEADS, Q_LORA, KV_LORA, NOPE, ROPE_D, C_V = 8, 256, 128, 128, 64, 128
MLA_SCALE = (NOPE + ROPE_D) ** -0.5
D_FF, FFN_CONV = 2816, 3
ROPE_THETA = 10000.0
EPS = 1e-6
AB_WIDTHS = (512, 512, 512, 4, 4, 512, 256, 256, 256, 64, 4)
AB_PACKED = 2944
INT_MIN = -(2 ** 31)
NEG = -1e30
LOG2E = 1.4426950408889634
VMEM_LIMIT = 56 * 1024 * 1024

_NT = (((1,), (1,)), ((), ()))


def _cp(sem):
    return pltpu.CompilerParams(dimension_semantics=sem, vmem_limit_bytes=VMEM_LIMIT)


def _dot(a, b):
    return jnp.dot(a, b, preferred_element_type=F32)


def _dot_nt(a, b):
    return lax.dot_general(a, b, _NT, preferred_element_type=F32)


def _rms(x, g):
    return x * lax.rsqrt(jnp.mean(x * x, axis=-1, keepdims=True) + EPS) * g


def _rope_piece(p, cos, sin, d):
    if d == 128:
        rot = pltpu.roll(p, 64, axis=1)
    else:
        lane = lax.broadcasted_iota(I32, p.shape, 1)
        rot = jnp.where((lane & 63) < 32, pltpu.roll(p, 96, axis=1), pltpu.roll(p, 32, axis=1))
    return p * cos + rot * sin


def _rope_wide(x, cos, sin, d):
    n = x.shape[1] // 128
    return jnp.concatenate([_rope_piece(x[:, i * 128:(i + 1) * 128], cos, sin, d) for i in range(n)], axis=1)


KEY_OF_NEG_INF = -2139095041


def _key_to_score(key):
    bits = key ^ ((key >> 31) & 0x7FFFFFFF)
    return jnp.where(key >= KEY_OF_NEG_INF, pltpu.bitcast(bits, F32), -jnp.inf)


def _lane_fold(x, op):
    out = x[:, 0:128]
    for u in range(1, x.shape[1] // 128):
        out = op(out, x[:, u * 128:(u + 1) * 128])
    return out


def _topk_threshold(count, vec_shape, k, idx_bits, key_axis):
    sl = lambda a, rs: a if rs is None else a[rs]
    t0 = jnp.where(count(lambda sc, i0, rs: sc >= 0.0) >= k, 0, INT_MIN).astype(I32)

    def bit_body(i, t):
        cand_key = t | jnp.left_shift(jnp.int32(1), 30 - i)
        cand = _key_to_score(cand_key)
        return jnp.where(count(lambda sc, i0, rs: sc >= sl(cand, rs)) >= k, cand_key, t)

    thr = _key_to_score(lax.fori_loop(0, 31, bit_body, t0))
    n_gt = count(lambda sc, i0, rs: sc > sl(thr, rs))
    n_eq = count(lambda sc, i0, rs: sc == sl(thr, rs))
    need = k - n_gt
    tie = jnp.max(jnp.where((n_eq > need) & (thr > -jnp.inf), 1, 0)) > 0

    def search():
        def idx_body(i, p):
            cand = p | jnp.left_shift(jnp.int32(1), idx_bits - 1 - i)

            def pred(kc, i0, rs):
                idx = i0 + lax.broadcasted_iota(I32, kc.shape, key_axis)
                return (kc == sl(thr, rs)) & (idx < sl(cand, rs))
            return jnp.where(count(pred) < need, cand, p)
        return lax.fori_loop(0, idx_bits, idx_body, jnp.zeros(vec_shape, I32))

    cut = lax.cond(tie, search, lambda: jnp.full(vec_shape, 2 ** 31 - 1, I32))
    return thr, cut


def _selected(sc, idx, thr, cut):
    return (sc > thr) | ((sc == thr) & (idx <= cut) & (thr > -jnp.inf))


def _flash_init(m_ref, l_ref, acc_ref):
    m_ref[...] = jnp.full(m_ref.shape, NEG, F32)
    l_ref[...] = jnp.zeros(l_ref.shape, F32)
    acc_ref[...] = jnp.zeros(acc_ref.shape, F32)


def _flash_update(s2, v, m_ref, l_ref, acc_ref, keep=None):
    if keep is not None:
        s2 = jnp.where(keep, s2, NEG)
    m_old = m_ref[...]
    m_new = jnp.maximum(m_old, jnp.max(_lane_fold(s2, jnp.maximum), axis=1, keepdims=True))
    alpha = jnp.exp2(m_old - m_new)
    n = s2.shape[1] // 128
    tiles = [jnp.exp2(s2[:, u * 128:(u + 1) * 128] - m_new) for u in range(n)]
    if keep is not None:
        tiles = [jnp.where(keep[:, u * 128:(u + 1) * 128], t, 0.0) for u, t in enumerate(tiles)]
    lsum = tiles[0]
    for t in tiles[1:]:
        lsum = lsum + t
    l_ref[...] = alpha * l_ref[...] + lsum
    p = jnp.concatenate(tiles, axis=1).astype(BF16)
    acc_ref[...] = alpha * acc_ref[...] + _dot(p, v)
    m_ref[...] = m_new


def _flash_finish(l_ref, acc_ref):
    return acc_ref[...] / jnp.sum(l_ref[...], axis=1, keepdims=True)


def _ab_proj_kernel(x_ref, g_ref, w_ref, c128_ref, s128_ref, c64_ref, s64_ref,
                    qk_ref, v_ref, o_ref, qa_ref, ka_ref, kab_ref, va_ref, vab_ref,
                    qi_ref, ki_ref, kib_ref, gt_ref):
    h = _rms(x_ref[...], g_ref[...]).astype(BF16)

    def seg(a, b):
        return _dot(h, w_ref[:, a:b])

    qk_ref[...] = seg(0, 512)
    v_ref[...] = seg(512, 1024)
    o_ref[...] = seg(1024, 1536)
    c128, s128, c64, s64 = c128_ref[...], s128_ref[...], c64_ref[...], s64_ref[...]
    qa_ref[...] = _rope_wide(seg(1536, 2048), c128, s128, 128).astype(BF16)
    tm = x_ref.shape[0]
    ka = _rope_wide(seg(2048, 2304), c128, s128, 128)
    kab_ref[...] = ka.astype(BF16)
    va = seg(2304, 2560)
    vab_ref[...] = va.astype(BF16)
    for gi in range(A_KV):
        ka_ref[pl.ds(gi, tm, stride=A_KV), :] = ka[:, gi * A_DIM:(gi + 1) * A_DIM]
        va_ref[pl.ds(gi, tm, stride=A_KV), :] = va[:, gi * A_DIM:(gi + 1) * A_DIM]
    qi_ref[...] = _rope_wide(seg(2560, 2816), c64, s64, 64).astype(BF16)
    last = seg(2816, 2944)
    ki = _rope_piece(last, c64, s64, 64)[:, :64]
    ki_ref[...] = ki
    kib_ref[...] = ki.astype(BF16)
    gt_ref[...] = last[:, 64:]


def _layer_spec(w, layer):
    zeros = (0,) * (w.ndim - 1)
    return pl.BlockSpec((None,) + w.shape[1:], lambda *_: (layer,) + zeros)


def _ab_proj(x, g, w, layer, tabs, tm):
    m = x.shape[0]
    c128, s128, c64, s64 = tabs
    tb = c128.shape[0] // tm
    row = lambda n: pl.BlockSpec((tm, n), lambda i: (i, 0))
    tab = pl.BlockSpec((tm, 128), lambda i: (i % tb, 0))
    full = lambda a: pl.BlockSpec(a.shape, lambda i: (0,) * a.ndim)
    widths = (512, 512, 512, 512, 256, 256, 256, 256, 256, 64, 64, 64)
    dtypes = (F32, F32, F32, BF16, F32, BF16, F32, BF16, BF16, F32, BF16, F32)
    out_specs = [row(n) for n in widths]
    out_shape = [jax.ShapeDtypeStruct((m, n), dt) for n, dt in zip(widths, dtypes)]
    for o in (4, 6):
        out_specs[o] = pl.BlockSpec((A_KV * tm, A_DIM), lambda i: (i, 0))
        out_shape[o] = jax.ShapeDtypeStruct((A_KV * m, A_DIM), F32)
    return pl.pallas_call(
        _ab_proj_kernel,
        grid=(m // tm,),
        in_specs=[row(D_MODEL), full(g), _layer_spec(w, layer), tab, tab, tab, tab],
        out_specs=out_specs,
        out_shape=out_shape,
        compiler_params=_cp(("parallel",)),
    )(x, g, w, c128, s128, c64, s64)


def _conv_silu_kernel(x_ref, buf_ref, w_ref, sc_ref, o_ref, work, *, tm, hb, stride, taps):
    @pl.when(pl.program_id(1) == 0)
    def _():
        work[0:hb, :] = buf_ref[0]

    x = x_ref[...]
    work[hb:hb + tm, :] = x
    y = x * w_ref[taps - 1:taps, :]
    for j in range(taps - 1):
        y = y + work[pl.ds(hb - (taps - 1 - j) * stride, tm), :] * w_ref[j:j + 1, :]
    o_ref[...] = y * jax.nn.sigmoid(y) * sc_ref[...]
    work[0:hb, :] = x[tm - hb:, :]


def _conv_silu(x, bufp, w, scale, groups, tm, stride):
    m, c = x.shape
    hb = bufp.shape[1]
    nt = m // groups // tm
    taps = w.shape[0]
    return pl.pallas_call(
        functools.partial(_conv_silu_kernel, tm=tm, hb=hb, stride=stride, taps=taps),
        grid=(groups, nt),
        in_specs=[pl.BlockSpec((tm, c), lambda g, i: (g * nt + i, 0)),
                  pl.BlockSpec((1, hb, c), lambda g, i: (g, 0, 0)),
                  pl.BlockSpec(w.shape, lambda g, i: (0, 0)),
                  pl.BlockSpec(scale.shape, lambda g, i: (0, 0))],
        out_specs=pl.BlockSpec((tm, c), lambda g, i: (g * nt + i, 0)),
        out_shape=jax.ShapeDtypeStruct((m, c), F32),
        scratch_shapes=[pltpu.VMEM((hb + tm, c), F32)],
        compiler_params=_cp(("arbitrary", "arbitrary")),
    )(x, bufp, w, scale)


def _log_sigmoid(x):
    return jnp.minimum(x, 0.0) - jnp.log1p(jnp.exp(-jnp.abs(x)))


def _mlstm_kernel(qk_ref, v_ref, o_ref, gt_ref, gtt_ref, brow_ref, bcol_ref, gh_ref,
                  c0_ref, n0_ref, m0_ref, h_ref, c1_ref, n1_ref, m1_ref, c_s, n_s, m_s, *, cl, nc, bb):
    c = pl.program_id(1)

    @pl.when(c == 0)
    def _():
        for bi in range(bb):
            for h in range(M_HEADS):
                c_s[bi * M_HEADS + h] = c0_ref[bi, h]
                n_s[bi * M_HEADS + h] = n0_ref[bi, h:h + 1, :]
                m_s[bi * M_HEADS + h] = m0_ref[bi, :, h:h + 1]

    r_i = lax.broadcasted_iota(I32, (cl, cl), 0)
    c_i = lax.broadcasted_iota(I32, (cl, cl), 1)
    causal = c_i <= r_i
    for bi in range(bb):
        _mlstm_chunk(bi, qk_ref[bi], v_ref[bi], o_ref[bi], gt_ref[bi] + brow_ref[...],
                     gtt_ref[bi] + bcol_ref[...], gh_ref, h_ref, c_s, n_s, m_s, causal, r_i, c_i, cl)

    @pl.when(c == nc - 1)
    def _():
        for bi in range(bb):
            for h in range(M_HEADS):
                c1_ref[bi, h] = c_s[bi * M_HEADS + h]
                n1_ref[bi, h:h + 1, :] = n_s[bi * M_HEADS + h]
                m1_ref[bi, :, h:h + 1] = m_s[bi * M_HEADS + h]


def _mlstm_chunk(bi, qk, vv, og, gates, gates_t, gh_ref, h_ref, c_s, n_s, m_s, causal, r_i, c_i, cl):
    li_col = gates[:, 0:4]
    lf_col = _log_sigmoid(gates[:, 4:8])
    li_row = gates_t[0:4, :]
    lf_row = _log_sigmoid(gates_t[4:8, :])
    outs = []
    for h in range(M_HEADS):
        sh = bi * M_HEADS + h
        q = qk[:, h * M_QK:(h + 1) * M_QK]
        k = qk[:, 256 + h * M_QK:256 + (h + 1) * M_QK]
        v = vv[:, h * M_V:(h + 1) * M_V]
        cm = c_s[sh]
        n = n_s[sh]
        m_prev = m_s[sh]
        b_col = jnp.sum(jnp.where(causal, lf_row[h:h + 1, :], 0.0), axis=1, keepdims=True)
        b_row = jnp.sum(jnp.where(r_i <= c_i, lf_col[:, h:h + 1], 0.0), axis=0, keepdims=True)
        dmat = jnp.where(causal, b_col - b_row + li_row[h:h + 1, :], -jnp.inf)
        inter = b_col + m_prev
        m_t = jnp.maximum(inter, jnp.max(dmat, axis=1, keepdims=True))
        iw = jnp.exp(inter - m_t)
        qb, kb, vb = q.astype(BF16), k.astype(BF16), v.astype(BF16)
        s = _dot_nt(qb, kb) * jnp.exp(dmat - m_t)
        num = iw * _dot(qb, cm.astype(BF16)) + _dot(s.astype(BF16), vb)
        den = iw * jnp.sum(q * n, axis=1, keepdims=True) + jnp.sum(s, axis=1, keepdims=True)
        hh = num / jnp.maximum(jnp.abs(den), jnp.exp(-m_t))
        m_new = m_t[cl - 1:cl, :]
        b_last = b_col[cl - 1:cl, :]
        w_end = jnp.exp(b_last - b_col + li_col[:, h:h + 1] - m_new)
        decay = jnp.exp(b_last + m_prev - m_new)
        kw = k * w_end
        c_s[sh] = decay * cm + lax.dot_general(kw.astype(BF16), vb, (((0,), (0,)), ((), ())),
                                               preferred_element_type=F32)
        n_s[sh] = decay * n + jnp.sum(kw, axis=0, keepdims=True)
        m_s[sh] = m_new
        hn = _rms(hh, gh_ref[:, h * M_V:(h + 1) * M_V])
        outs.append(hn * jax.nn.sigmoid(og[:, h * M_V:(h + 1) * M_V]))
    h_ref[bi] = jnp.concatenate(outs, axis=1)


def _mlstm(qkc, v, o, gates, gates_t, b_row, b_col, g_head, c0, n0, m0, batch, cl, bb):
    t = qkc.shape[1]
    nc = t // cl
    row = lambda n: pl.BlockSpec((bb, cl, n), lambda b, c: (b, c, 0))
    full = lambda a: pl.BlockSpec(a.shape, lambda b, c: (0,) * a.ndim)
    st_c = pl.BlockSpec((bb, M_HEADS, M_QK, M_V), lambda b, c: (b, 0, 0, 0))
    st_n = pl.BlockSpec((bb, M_HEADS, M_QK), lambda b, c: (b, 0, 0))
    st_m = pl.BlockSpec((bb, 1, M_HEADS), lambda b, c: (b, 0, 0))
    return pl.pallas_call(
        functools.partial(_mlstm_kernel, cl=cl, nc=nc, bb=bb),
        grid=(batch // bb, nc),
        in_specs=[row(512), row(512), row(512), row(64),
                  pl.BlockSpec((bb, 8, cl), lambda b, c: (b, 0, c)),
                  full(b_row), full(b_col), full(g_head), st_c, st_n, st_m],
        out_specs=[row(512), st_c, st_n, st_m],
        out_shape=[jax.ShapeDtypeStruct((batch, t, 512), F32),
                   jax.ShapeDtypeStruct((batch, M_HEADS, M_QK, M_V), F32),
                   jax.ShapeDtypeStruct((batch, M_HEADS, M_QK), F32),
                   jax.ShapeDtypeStruct((batch, 1, M_HEADS), F32)],
        scratch_shapes=[pltpu.VMEM((bb * M_HEADS, M_QK, M_V), F32),
                        pltpu.VMEM((bb * M_HEADS, 1, M_QK), F32),
                        pltpu.VMEM((bb * M_HEADS, 1, 1), F32)],
        compiler_params=_cp(("arbitrary", "arbitrary")),
    )(qkc, v, o, gates, gates_t, b_row, b_col, g_head, c0, n0, m0)


def _dsa_prompt_kernel(qa_ref, qi_ref, gtt_ref, kab_ref, vab_ref, kib_ref, o_ref,
                       keys_t, q_s, m_s, l_s, acc_s, *, tq, cw, topk, idx_bits):
    i = pl.program_id(1)
    n_ch = (i * tq + tq + cw - 1) // cw
    qpos = i * tq + lax.broadcasted_iota(I32, (1, tq), 1)
    wi = gtt_ref[0][8:12, :] * IDX_SCALE
    qi = qi_ref[...]

    def score_body(c, carry):
        off = pl.multiple_of(c * cw, cw)
        kc = kib_ref[pl.ds(off, cw), :]
        sc = jnp.zeros((cw, tq), F32)
        for h in range(IDX_HEADS):
            rel = jnp.maximum(_dot_nt(kc, qi[:, h * IDX_DIM:(h + 1) * IDX_DIM]), 0.0)
            sc = sc + rel * wi[h:h + 1, :]
        kpos = off + lax.broadcasted_iota(I32, (cw, 1), 0)
        keys_t[c] = jnp.where(kpos <= qpos, sc, -jnp.inf)
        return carry

    lax.fori_loop(0, n_ch, score_body, 0)

    def count(pred):
        def body(c, acc):
            p = pred(keys_t[c], c * cw, None).astype(F32)
            return acc + jnp.sum(p.reshape(cw // 32, 32, tq), axis=0)
        acc = lax.fori_loop(0, n_ch, body, jnp.zeros((32, tq), F32))
        return jnp.sum(acc, axis=0, keepdims=True)

    thr, cut = _topk_threshold(count, (1, tq), topk, idx_bits, key_axis=0)

    qa = qa_ref[...]
    for g in range(A_KV):
        q_s[g] = jnp.concatenate([qa[:, (2 * g) * A_DIM:(2 * g + 1) * A_DIM],
                                  qa[:, (2 * g + 1) * A_DIM:(2 * g + 2) * A_DIM]], axis=0)
        _flash_init(m_s.at[g], l_s.at[g], acc_s.at[g])

    def att_body(c, carry):
        off = pl.multiple_of(c * cw, cw)
        idx = off + lax.broadcasted_iota(I32, (cw, tq), 0)
        bias = jnp.where(_selected(keys_t[c], idx, thr, cut), 0.0, NEG).T
        bias2 = jnp.concatenate([bias, bias], axis=0)
        for g in range(A_KV):
            kg = kab_ref[pl.ds(off, cw), g * A_DIM:(g + 1) * A_DIM]
            vg = vab_ref[pl.ds(off, cw), g * A_DIM:(g + 1) * A_DIM]
            s2 = _dot_nt(q_s[g], kg) * (A_DIM ** -0.5 * LOG2E) + bias2
            _flash_update(s2, vg, m_s.at[g], l_s.at[g], acc_s.at[g])
        return carry

    lax.fori_loop(0, n_ch, att_body, 0)
    outs = []
    for g in range(A_KV):
        og = _flash_finish(l_s.at[g], acc_s.at[g])
        outs += [og[0:tq], og[tq:2 * tq]]
    o_ref[...] = jnp.concatenate(outs, axis=1)


def _dsa_prompt(qa, qi, gates_t, kab, vab, kib, batch, topk, tq):
    m = qa.shape[0]
    t = m // batch
    nq = t // tq
    cw = min(512, t)
    row = lambda n: pl.BlockSpec((tq, n), lambda b, i: (b * nq + i, 0))
    whole = lambda n: pl.BlockSpec((t, n), lambda b, i: (b, 0))
    return pl.pallas_call(
        functools.partial(_dsa_prompt_kernel, tq=tq, cw=cw, topk=topk,
                          idx_bits=max(1, (t - 1).bit_length())),
        grid=(batch, nq),
        in_specs=[row(512), row(256), pl.BlockSpec((1, 16, tq), lambda b, i: (b, 0, i)),
                  whole(256), whole(256), whole(64)],
        out_specs=row(512),
        out_shape=jax.ShapeDtypeStruct((m, 512), F32),
        scratch_shapes=[pltpu.VMEM((t // cw, cw, tq), F32),
                        pltpu.VMEM((A_KV, 2 * tq, A_DIM), BF16),
                        pltpu.VMEM((A_KV, 2 * tq, 128), F32),
                        pltpu.VMEM((A_KV, 2 * tq, 128), F32),
                        pltpu.VMEM((A_KV, 2 * tq, A_DIM), F32)],
        compiler_params=_cp(("parallel", "arbitrary")),
    )(qa, qi, gates_t, kab, vab, kib)


def _idx_scores(qi, gates, keys_mat, keys_on_lanes):
    qs = jnp.concatenate([qi[:, h * IDX_DIM:(h + 1) * IDX_DIM] for h in range(IDX_HEADS)], axis=0)
    ws = jnp.concatenate([gates[:, 8 + h:9 + h] for h in range(IDX_HEADS)], axis=0) * IDX_SCALE
    qk = _dot(qs, keys_mat) if keys_on_lanes else _dot_nt(qs, keys_mat)
    rel = jnp.maximum(qk, 0.0) * ws
    t = qi.shape[0]
    sc = rel[0:t]
    for h in range(1, IDX_HEADS):
        sc = sc + rel[h * t:(h + 1) * t]
    return sc


def _dsa_scores_sample_kernel(pt_ref, qi_ref, gt_ref, knew_ref, *rest, pp):
    pages = rest[:pp]
    kp_ref, kn_ref = rest[pp], rest[pp + 1]
    qi = qi_ref[...]
    gates = gt_ref[...]
    kcat = jnp.concatenate([p[...] for p in pages], axis=1).astype(BF16)
    kp_ref[0] = _idx_scores(qi, gates, kcat, True)
    sn = _idx_scores(qi, gates, knew_ref[0], False)
    t = qi.shape[0]
    vis = lax.broadcasted_iota(I32, (t, PAGE), 1) <= lax.broadcasted_iota(I32, (t, PAGE), 0)
    kn_ref[0] = jnp.where(vis, sn, -jnp.inf)


def _page_specs(shape_tail, layer, n_pages, pp):
    nd = len(shape_tail)

    def spec(u):
        return pl.BlockSpec((None, None) + shape_tail,
                            lambda b, j, pt: (layer, pt[b * n_pages + j * pp + u]) + (0,) * nd)
    return [spec(u) for u in range(pp)]


def _dsa_scores_sample(pt, qi, gates, knew, pool, layer, batch, tt, n_pages, pp):
    grid_spec = pltpu.PrefetchScalarGridSpec(
        num_scalar_prefetch=1,
        grid=(batch, n_pages // pp),
        in_specs=[pl.BlockSpec((tt, 256), lambda b, j, pt: (b, 0)),
                  pl.BlockSpec((tt, 64), lambda b, j, pt: (b, 0)),
                  pl.BlockSpec((1, PAGE, IDX_DIM), lambda b, j, pt: (b, 0, 0))]
                 + _page_specs((IDX_DIM, PAGE), layer, n_pages, pp),
        out_specs=[pl.BlockSpec((1, tt, pp * PAGE), lambda b, j, pt: (b, 0, j)),
                   pl.BlockSpec((1, tt, PAGE), lambda b, j, pt: (b, 0, 0))],
    )
    return pl.pallas_call(
        functools.partial(_dsa_scores_sample_kernel, pp=pp),
        grid_spec=grid_spec,
        out_shape=[jax.ShapeDtypeStruct((batch, tt, n_pages * PAGE), F32),
                   jax.ShapeDtypeStruct((batch, tt, PAGE), F32)],
        compiler_params=_cp(("parallel", "arbitrary")),
    )(pt, qi, gates, knew, *([pool] * pp))


def _thresh_sample_kernel(kp_ref, kn_ref, thr_ref, cut_ref, *, rows, cw, n_past, topk, idx_bits):
    def count(pred):
        tt = kp_ref.shape[1]
        rb = min(128, rows)
        parts = []
        for r0 in range(0, rows, rb):
            rs = slice(r0, r0 + rb)
            bs = slice(r0 // tt, (r0 + rb) // tt)
            acc = pred(kn_ref[bs].reshape(rb, PAGE), n_past * cw, rs).astype(F32)
            for c in range(n_past):
                kc = kp_ref[bs, :, c * cw:(c + 1) * cw].reshape(rb, cw)
                acc = acc + _lane_fold(pred(kc, c * cw, rs).astype(F32), jnp.add)
            parts.append(acc)
        return jnp.sum(jnp.concatenate(parts, axis=0), axis=1, keepdims=True)

    thr, cut = _topk_threshold(count, (rows, 1), topk, idx_bits, key_axis=1)
    thr_ref[...] = jnp.broadcast_to(thr, (rows, PAGE)).reshape(thr_ref.shape)
    cut_ref[...] = jnp.broadcast_to(cut, (rows, PAGE)).reshape(cut_ref.shape)


def _thresh_sample(keys_past, keys_new, topk, bg):
    batch, tt, lp = keys_past.shape
    cw = 512
    rows = bg * tt
    n_past = lp // cw
    blk = lambda n: pl.BlockSpec((bg, tt, n), lambda i: (i, 0, 0))
    return pl.pallas_call(
        functools.partial(_thresh_sample_kernel, rows=rows, cw=cw, n_past=n_past, topk=topk,
                          idx_bits=(lp + PAGE - 1).bit_length()),
        grid=(batch // bg,),
        in_specs=[blk(lp), blk(PAGE)],
        out_specs=[blk(PAGE), blk(PAGE)],
        out_shape=[jax.ShapeDtypeStruct((batch, tt, PAGE), F32), jax.ShapeDtypeStruct((batch, tt, PAGE), I32)],
        compiler_params=_cp(("parallel",)),
    )(keys_past, keys_new)


def _dsa_attend_sample_kernel(pt_ref, qa_ref, thr_ref, cut_ref, kp_ref, kn_ref, knew_ref, vnew_ref, *rest,
                              pp, n_steps, n_past_keys):
    kpages, vpages = rest[:pp], rest[pp:2 * pp]
    o_ref = rest[2 * pp]
    m_s, l_s, acc_s = rest[2 * pp + 1:]
    j = pl.program_id(1)
    tt = qa_ref.shape[0]

    @pl.when(j == 0)
    def _():
        for g in range(A_KV):
            _flash_init(m_s.at[g], l_s.at[g], acc_s.at[g])

    qa = qa_ref[...]
    qg = [jnp.concatenate([qa[:, (2 * g) * A_DIM:(2 * g + 1) * A_DIM],
                           qa[:, (2 * g + 1) * A_DIM:(2 * g + 2) * A_DIM]], axis=0) for g in range(A_KV)]
    thr = thr_ref[0][:, 0:1]
    cut = cut_ref[0][:, 0:1]

    def attend(kc, first_idx, kv_of_group):
        idx = first_idx + lax.broadcasted_iota(I32, kc.shape, 1)
        bias = jnp.where(_selected(kc, idx, thr, cut), 0.0, NEG)
        bias2 = jnp.concatenate([bias, bias], axis=0)
        for g in range(A_KV):
            kg, vg = kv_of_group(g)
            s2 = _dot_nt(qg[g], kg) * (A_DIM ** -0.5 * LOG2E) + bias2
            _flash_update(s2, vg, m_s.at[g], l_s.at[g], acc_s.at[g])

    def paged(g):
        rows = lambda p: p[pl.ds(g, PAGE, stride=A_KV), :]
        return (jnp.concatenate([rows(p) for p in kpages], axis=0).astype(BF16),
                jnp.concatenate([rows(p) for p in vpages], axis=0).astype(BF16))

    attend(kp_ref[0], j * (pp * PAGE), paged)

    @pl.when(j == n_steps - 1)
    def _():
        attend(kn_ref[0], n_past_keys,
               lambda g: (knew_ref[0][:, g * A_DIM:(g + 1) * A_DIM], vnew_ref[0][:, g * A_DIM:(g + 1) * A_DIM]))
        outs = []
        for g in range(A_KV):
            og = _flash_finish(l_s.at[g], acc_s.at[g])
            outs += [og[0:tt], og[tt:2 * tt]]
        o_ref[...] = jnp.concatenate(outs, axis=1)


def _dsa_attend_sample(pt, qa, thr, cut, keys_past, keys_new, knew, vnew, pool_k, pool_v,
                       layer, batch, tt, n_pages, pp):
    n_steps = n_pages // pp
    kvw = A_KV * A_DIM
    grid_spec = pltpu.PrefetchScalarGridSpec(
        num_scalar_prefetch=1,
        grid=(batch, n_steps),
        in_specs=[pl.BlockSpec((tt, 512), lambda b, j, pt: (b, 0)),
                  pl.BlockSpec((1, tt, PAGE), lambda b, j, pt: (b, 0, 0)),
                  pl.BlockSpec((1, tt, PAGE), lambda b, j, pt: (b, 0, 0)),
                  pl.BlockSpec((1, tt, pp * PAGE), lambda b, j, pt: (b, 0, j)),
                  pl.BlockSpec((1, tt, PAGE), lambda b, j, pt: (b, 0, 0)),
                  pl.BlockSpec((1, PAGE, kvw), lambda b, j, pt: (b, 0, 0)),
                  pl.BlockSpec((1, PAGE, kvw), lambda b, j, pt: (b, 0, 0))]
                 + _page_specs((PAGE * A_KV, A_DIM), layer, n_pages, pp)
                 + _page_specs((PAGE * A_KV, A_DIM), layer, n_pages, pp),
        out_specs=pl.BlockSpec((tt, 512), lambda b, j, pt: (b, 0)),
        scratch_shapes=[pltpu.VMEM((A_KV, 2 * tt, 128), F32),
                        pltpu.VMEM((A_KV, 2 * tt, 128), F32),
                        pltpu.VMEM((A_KV, 2 * tt, A_DIM), F32)],
    )
    return pl.pallas_call(
        functools.partial(_dsa_attend_sample_kernel, pp=pp, n_steps=n_steps, n_past_keys=n_pages * PAGE),
        grid_spec=grid_spec,
        out_shape=jax.ShapeDtypeStruct((batch * tt, 512), F32),
        compiler_params=_cp(("parallel", "arbitrary")),
    )(pt, qa, thr, cut, keys_past, keys_new, knew, vnew, *([pool_k] * pp), *([pool_v] * pp))


def _mla_pre_kernel(x_ref, g_ref, win_ref, gq_ref, gkv_ref, wuq_ref, wuk_ref, c64_ref, s64_ref,
                    ckv_ref, kr_ref, kcat_ref, qcat_ref):
    h = _rms(x_ref[...], g_ref[...]).astype(BF16)
    cq = _rms(_dot(h, win_ref[:, 0:Q_LORA]), gq_ref[...])
    rest = _dot(h, win_ref[:, Q_LORA:Q_LORA + 256])
    ckv = _rms(rest[:, 0:KV_LORA], gkv_ref[...])
    ckv_ref[...] = ckv
    c64, s64 = c64_ref[...], s64_ref[...]
    lane = lax.broadcasted_iota(I32, (h.shape[0], 128), 1)
    kr = jnp.where(lane < ROPE_D, _rope_piece(rest[:, 128:256], c64, s64, 64), 0.0)
    kr_ref[...] = kr[:, :ROPE_D]
    kcat_ref[...] = jnp.concatenate([ckv, kr], axis=1).astype(BF16)
    q = _dot(cq.astype(BF16), wuq_ref[...])
    qr = _rope_wide(q[:, 1024:1536], c64, s64, 64)
    pieces = []
    for hh in range(C_HEADS):
        pieces.append(_dot(q[:, hh * NOPE:(hh + 1) * NOPE].astype(BF16), wuk_ref[hh]))
        pair = qr[:, (hh // 2) * 128:(hh // 2 + 1) * 128]
        if hh % 2:
            pair = pltpu.roll(pair, ROPE_D, axis=1)
        pieces.append(jnp.where(lane < ROPE_D, pair, 0.0))
    qcat_ref[...] = jnp.concatenate(pieces, axis=1).astype(BF16)


def _mla_pre(x, g, win, gq, gkv, wuq, wuk, layer, tabs, tm):
    m = x.shape[0]
    c64, s64 = tabs[2], tabs[3]
    tb = c64.shape[0] // tm
    row = lambda n: pl.BlockSpec((tm, n), lambda i: (i, 0))
    tab = pl.BlockSpec((tm, 128), lambda i: (i % tb, 0))
    full = lambda a: pl.BlockSpec(a.shape, lambda i: (0,) * a.ndim)
    widths = (KV_LORA, ROPE_D, 256, C_HEADS * 256)
    dtypes = (F32, F32, BF16, BF16)
    return pl.pallas_call(
        _mla_pre_kernel,
        grid=(m // tm,),
        in_specs=[row(D_MODEL), full(g), _layer_spec(win, layer), full(gq), full(gkv), _layer_spec(wuq, layer),
                  _layer_spec(wuk, layer), tab, tab],
        out_specs=[row(n) for n in widths],
        out_shape=[jax.ShapeDtypeStruct((m, n), dt) for n, dt in zip(widths, dtypes)],
        compiler_params=_cp(("parallel",)),
    )(x, g, win, gq, gkv, wuq, wuk, c64, s64)


def _stack_heads(qcat):
    return jnp.concatenate([qcat[:, h * 256:(h + 1) * 256] for h in range(C_HEADS)], axis=0)


def _unstack_heads(o, t):
    return jnp.concatenate([o[h * t:(h + 1) * t] for h in range(C_HEADS)], axis=1)


def _mla_prompt_kernel(q_ref, k_ref, o_ref, q_s, m_s, l_s, acc_s, *, tq):
    i = pl.program_id(1)
    q_s[...] = _stack_heads(q_ref[...])
    _flash_init(m_s, l_s, acc_s)

    def step(c, keep):
        kc = k_ref[pl.ds(pl.multiple_of(c * tq, tq), tq), :]
        s2 = _dot_nt(q_s[...], kc) * (MLA_SCALE * LOG2E)
        _flash_update(s2, kc[:, 0:KV_LORA], m_s, l_s, acc_s, keep=keep)

    def body(c, carry):
        step(c, None)
        return carry

    lax.fori_loop(0, i, body, 0)
    rows = C_HEADS * tq
    t_in = lax.broadcasted_iota(I32, (rows, tq), 0) & (tq - 1)
    step(i, lax.broadcasted_iota(I32, (rows, tq), 1) <= t_in)
    o_ref[...] = _unstack_heads(_flash_finish(l_s, acc_s), tq).astype(BF16)


def _mla_prompt(qcat, kcat, batch, tq):
    m = qcat.shape[0]
    t = m // batch
    nq = t // tq
    return pl.pallas_call(
        functools.partial(_mla_prompt_kernel, tq=tq),
        grid=(batch, nq),
        in_specs=[pl.BlockSpec((tq, C_HEADS * 256), lambda b, i: (b * nq + i, 0)),
                  pl.BlockSpec((t, 256), lambda b, i: (b, 0))],
        out_specs=pl.BlockSpec((tq, C_HEADS * KV_LORA), lambda b, i: (b * nq + i, 0)),
        out_shape=jax.ShapeDtypeStruct((m, C_HEADS * KV_LORA), BF16),
        scratch_shapes=[pltpu.VMEM((C_HEADS * tq, 256), BF16),
                        pltpu.VMEM((C_HEADS * tq, 128), F32),
                        pltpu.VMEM((C_HEADS * tq, 128), F32),
                        pltpu.VMEM((C_HEADS * tq, KV_LORA), F32)],
        compiler_params=_cp(("parallel", "arbitrary")),
    )(qcat, kcat)


def _mla_sample_kernel(pt_ref, q_ref, knew_ref, *rest, pp, n_steps):
    cpages, rpages = rest[:pp], rest[pp:2 * pp]
    o_ref = rest[2 * pp]
    m_s, l_s, acc_s = rest[2 * pp + 1:]
    j = pl.program_id(1)
    tt = q_ref.shape[0]

    @pl.when(j == 0)
    def _():
        _flash_init(m_s, l_s, acc_s)

    qs = _stack_heads(q_ref[...])
    ccat = jnp.concatenate([p[...] for p in cpages], axis=0).astype(BF16)
    rcat = jnp.concatenate([p[...] for p in rpages], axis=1).astype(BF16)
    s2 = (_dot_nt(qs[:, 0:KV_LORA], ccat) + _dot(qs[:, KV_LORA:KV_LORA + ROPE_D], rcat)) * (MLA_SCALE * LOG2E)
    _flash_update(s2, ccat, m_s, l_s, acc_s)

    @pl.when(j == n_steps - 1)
    def _():
        kn = knew_ref[0]
        rows = C_HEADS * tt
        t_in = lax.broadcasted_iota(I32, (rows, PAGE), 0) & (tt - 1)
        keep = lax.broadcasted_iota(I32, (rows, PAGE), 1) <= t_in
        _flash_update(_dot_nt(qs, kn) * (MLA_SCALE * LOG2E), kn[:, 0:KV_LORA], m_s, l_s, acc_s, keep=keep)
        o_ref[...] = _unstack_heads(_flash_finish(l_s, acc_s), tt).astype(BF16)


def _mla_sample(pt, qcat, knew, pool_c, pool_r, layer, batch, tt, n_pages, pp):
    n_steps = n_pages // pp
    grid_spec = pltpu.PrefetchScalarGridSpec(
        num_scalar_prefetch=1,
        grid=(batch, n_steps),
        in_specs=[pl.BlockSpec((tt, C_HEADS * 256), lambda b, j, pt: (b, 0)),
                  pl.BlockSpec((1, PAGE, 256), lambda b, j, pt: (b, 0, 0))]
                 + _page_specs((PAGE, KV_LORA), layer, n_pages, pp)
                 + _page_specs((ROPE_D, PAGE), layer, n_pages, pp),
        out_specs=pl.BlockSpec((tt, C_HEADS * KV_LORA), lambda b, j, pt: (b, 0)),
        scratch_shapes=[pltpu.VMEM((C_HEADS * tt, 128), F32),
                        pltpu.VMEM((C_HEADS * tt, 128), F32),
                        pltpu.VMEM((C_HEADS * tt, KV_LORA), F32)],
    )
    return pl.pallas_call(
        functools.partial(_mla_sample_kernel, pp=pp, n_steps=n_steps),
        grid_spec=grid_spec,
        out_shape=jax.ShapeDtypeStruct((batch * tt, C_HEADS * KV_LORA), BF16),
        compiler_params=_cp(("parallel", "arbitrary")),
    )(pt, qcat, knew, *([pool_c] * pp), *([pool_r] * pp))


def _ab_out_kernel(x_ref, hm_ref, oa_ref, w_ref, o_ref):
    o_ref[...] = (x_ref[...] + _dot(hm_ref[...].astype(BF16), w_ref[0:512, :])
                  + _dot(oa_ref[...].astype(BF16), w_ref[512:1024, :]))


def _ab_out(x, hm, oa, w, layer, tm):
    m = x.shape[0]
    row = lambda n: pl.BlockSpec((tm, n), lambda i: (i, 0))
    return pl.pallas_call(
        _ab_out_kernel,
        grid=(m // tm,),
        in_specs=[row(D_MODEL), row(512), row(512), _layer_spec(w, layer)],
        out_specs=row(D_MODEL),
        out_shape=jax.ShapeDtypeStruct((m, D_MODEL), F32),
        compiler_params=_cp(("parallel",)),
    )(x, hm, oa, w)


def _mla_out_kernel(x_ref, ol_ref, wuv_ref, w_ref, o_ref):
    ol = ol_ref[...]
    o = jnp.concatenate([_dot(ol[:, h * KV_LORA:(h + 1) * KV_LORA], wuv_ref[h]) for h in range(C_HEADS)],
                        axis=1)
    o_ref[...] = x_ref[...] + _dot(o.astype(BF16), w_ref[...])


def _mla_out(x, ol, wuv, w, layer, tm):
    m = x.shape[0]
    row = lambda n: pl.BlockSpec((tm, n), lambda i: (i, 0))
    return pl.pallas_call(
        _mla_out_kernel,
        grid=(m // tm,),
        in_specs=[row(D_MODEL), row(1024), _layer_spec(wuv, layer), _layer_spec(w, layer)],
        out_specs=row(D_MODEL),
        out_shape=jax.ShapeDtypeStruct((m, D_MODEL), F32),
        compiler_params=_cp(("parallel",)),
    )(x, ol, wuv, w)


def _load_tm(ref, lead, steps):
    if steps <= 1:
        return ref[lead + (slice(None), slice(None))]
    return jnp.concatenate([ref[:, s, :] for s in range(steps)], axis=0)


def _store_tm(ref, lead, val, steps):
    if steps <= 1:
        ref[lead + (slice(None), slice(None))] = val
        return
    nb = ref.shape[0]
    for s in range(steps):
        ref[:, s, :] = val[s * nb:(s + 1) * nb]


def _ffn_kernel(x_ref, g_ref, wa_ref, wg_ref, wc_ref, bc_ref, wd_ref, buf_ref, gf_ref,
                o_ref, st_ref, y_ref, h_s, acc_s, halo, work, *, tm, hb, stride, nj, final, xsteps, bsteps):
    i = pl.program_id(1)
    j = pl.program_id(2)

    @pl.when(j == 0)
    def _():
        h_s[...] = _rms(_load_tm(x_ref, (), xsteps), g_ref[...]).astype(BF16)
        acc_s[...] = jnp.zeros(acc_s.shape, F32)

    @pl.when(i == 0)
    def _():
        halo[j] = _load_tm(buf_ref, (0,), bsteps)

    h = h_s[...]
    a = _dot(h, wa_ref[...])
    gg = _dot(h, wg_ref[...])
    work[0:hb, :] = halo[j]
    work[hb:hb + tm, :] = gg
    gc = gg * wc_ref[FFN_CONV - 1:FFN_CONV, :]
    for t in range(FFN_CONV - 1):
        gc = gc + work[pl.ds(hb - (FFN_CONV - 1 - t) * stride, tm), :] * wc_ref[t:t + 1, :]
    last = gg[tm - hb:, :]
    halo[j] = last
    _store_tm(st_ref, (0, 0), last, bsteps)
    gc = gc + bc_ref[...]
    p = a * (gc * jax.nn.sigmoid(gc))
    acc_s[...] += _dot(p.astype(BF16), wd_ref[...])

    @pl.when(j == nj - 1)
    def _():
        xn = _load_tm(x_ref, (), xsteps) + acc_s[...]
        _store_tm(o_ref, (), xn, xsteps)
        if final:
            _store_tm(y_ref, (), _rms(xn, gf_ref[...]), xsteps)


def _ffn(x, g, w_up, wc, bc, wd, layer, bufp, g_final, groups, tm, fc, stride, final, xsteps=1, bsteps=1):
    nj = D_FF // fc
    vec = lambda a: pl.BlockSpec(a.shape, lambda gi, i, j: (0, 0))
    if xsteps > 1:
        nb = x.shape[0]
        assert groups == 1 and tm == nb * xsteps
        nt, hb = 1, nb * bsteps
        row = pl.BlockSpec((nb, xsteps, D_MODEL), lambda gi, i, j: (0, 0, 0))
        st = st_out = pl.BlockSpec((nb, bsteps, fc), lambda gi, i, j: (0, 0, j))
        st_shape = (nb, bsteps, D_FF)
    else:
        hb = bufp.shape[1]
        nt = x.shape[0] // groups // tm
        row = pl.BlockSpec((tm, D_MODEL), lambda gi, i, j: (gi * nt + i, 0))
        st = pl.BlockSpec((1, hb, fc), lambda gi, i, j: (gi, 0, j))
        st_out = pl.BlockSpec((1, 1, hb, fc), lambda gi, i, j: (gi, i, 0, j))
        st_shape = (groups, nt, hb, D_FF)
    outs = [row, st_out] + ([row] if final else [])
    shapes = [jax.ShapeDtypeStruct(x.shape, F32), jax.ShapeDtypeStruct(st_shape, F32)]
    if final:
        shapes.append(jax.ShapeDtypeStruct(x.shape, F32))

    static = dict(tm=tm, hb=hb, stride=stride, nj=nj, xsteps=xsteps, bsteps=bsteps)

    def kern(*refs):
        if final:
            return _ffn_kernel(*refs, final=True, **static)
        ins, rest = refs[:9], refs[9:]
        return _ffn_kernel(*ins, rest[0], rest[1], None, *rest[2:], final=False, **static)

    return pl.pallas_call(
        kern,
        grid=(groups, nt, nj),
        in_specs=[row, vec(g),
                  pl.BlockSpec((None, D_MODEL, fc), lambda gi, i, j: (layer, 0, j)),
                  pl.BlockSpec((None, D_MODEL, fc), lambda gi, i, j: (layer, 0, nj + j)),
                  pl.BlockSpec((FFN_CONV, fc), lambda gi, i, j: (0, j)),
                  pl.BlockSpec((1, fc), lambda gi, i, j: (0, j)),
                  pl.BlockSpec((None, fc, D_MODEL), lambda gi, i, j: (layer, j, 0)),
                  st, vec(g_final)],
        out_specs=outs,
        out_shape=shapes,
        scratch_shapes=[pltpu.VMEM((tm, D_MODEL), BF16),
                        pltpu.VMEM((tm, D_MODEL), F32),
                        pltpu.VMEM((nj, hb, fc), F32),
                        pltpu.VMEM((hb + tm, fc), F32)],
        compiler_params=_cp(("arbitrary", "arbitrary", "arbitrary")),
    )(x, g, w_up, w_up, wc, bc, wd, bufp, g_final)


def _rope_tables(pos, d):
    half = d // 2
    inv = ROPE_THETA ** (-jnp.arange(half, dtype=F32) * (2.0 / d))
    ang = pos.astype(F32)[:, None] * inv[None, :]
    cos, sin = jnp.cos(ang), jnp.sin(ang)
    reps = 128 // d
    return (jnp.tile(jnp.concatenate([cos, cos], axis=1), (1, reps)),
            jnp.tile(jnp.concatenate([-sin, sin], axis=1), (1, reps)))


def _prep_weights(w_in_ab, w_out_ab, w_in_mla, w_uq, w_uk, w_uv, w_out_mla, w_up, w_down):
    cuts = np.cumsum((0,) + AB_WIDTHS)
    seg = lambda w, k: w[:, :, cuts[k]:cuts[k + 1]]
    n_ab = w_in_ab.shape[0]
    pad = jnp.zeros((n_ab, D_MODEL, 52), F32)
    w_ab = jnp.concatenate([seg(w_in_ab, 0), seg(w_in_ab, 1), seg(w_in_ab, 2), seg(w_in_ab, 5),
                            seg(w_in_ab, 6), seg(w_in_ab, 7), seg(w_in_ab, 8), seg(w_in_ab, 9),
                            seg(w_in_ab, 3), seg(w_in_ab, 4), seg(w_in_ab, 10), pad], axis=2).astype(BF16)
    n_c = w_in_mla.shape[0]
    w_mla = jnp.concatenate([w_in_mla, jnp.zeros((n_c, D_MODEL, 64), F32)], axis=2).astype(BF16)
    uq = w_uq.reshape(n_c, Q_LORA, C_HEADS, NOPE + ROPE_D)
    w_uq2 = jnp.concatenate([uq[..., :NOPE].reshape(n_c, Q_LORA, C_HEADS * NOPE),
                             uq[..., NOPE:].reshape(n_c, Q_LORA, C_HEADS * ROPE_D)], axis=2).astype(BF16)
    return dict(w_ab=w_ab, w_out_ab=w_out_ab.astype(BF16), w_mla=w_mla, w_uq=w_uq2,
                w_uk=w_uk.astype(BF16), w_uv=w_uv.astype(BF16), w_out_mla=w_out_mla.astype(BF16),
                w_up=w_up.astype(BF16), w_down=w_down.astype(BF16))


def _front_pad(buf, hb):
    g, r, c = buf.shape
    return jnp.concatenate([jnp.zeros((g, hb - r, c), F32), buf], axis=1)


def _trunk(x, pos_rows, P, W, past, cfg):
    batch, tt = cfg["batch"], cfg["t"]
    tm, tq_dsa, tq_mla = cfg["tm"], cfg["tq_dsa"], cfg["tq_mla"]
    m = batch * tt
    c128, s128 = _rope_tables(pos_rows, 128)
    c64, s64 = _rope_tables(pos_rows, 64)
    tabs = (c128, s128, c64, s64)
    kscale = jnp.concatenate([jnp.ones((1, 256), F32), jnp.full((1, 256), M_QK ** -0.5, F32)], axis=1)
    ab_states, c_states, ffn_states = [], [], []
    y_final = None
    for l in range(DEPTH):
        j = l // 2
        g_attn = P["g_attn"][l][None, :]
        if l % 2 == 0:
            (qk_m, v_m, o_m, qa, ka, kab, va, vab, qi, ki, kib, gates) = _ab_proj(x, g_attn, W["w_ab"], j, tabs, tm)
            if past is None:
                conv_buf = jnp.zeros((batch, M_CONV - 1, 512), F32)
                c0 = jnp.zeros((batch, M_HEADS, M_QK, M_V), F32)
                n0 = jnp.zeros((batch, M_HEADS, M_QK), F32)
                m0 = jnp.zeros((batch, 1, M_HEADS), F32)
            else:
                conv_buf = past["mconv"][j]
                c0, n0, m0 = past["C"][j], past["n"][j], past["m"][j][:, None, :]
            qkc = _conv_silu(qk_m, _front_pad(conv_buf, 8), P["w_mconv"][j], kscale,
                             groups=batch, tm=min(tm, tt), stride=1)
            cl = math.gcd(tt, M_CHUNK)
            gates_t = jnp.transpose(gates[:, :16].reshape(batch, tt, 16), (0, 2, 1))
            bias = jnp.concatenate([P["b_igate"][j], P["b_fgate"][j]])
            b_row = jnp.concatenate([bias, jnp.zeros((56,), F32)])[None, :]
            b3 = lambda a: a.reshape(batch, tt, a.shape[-1])
            h_m, c1, n1, m1 = _mlstm(b3(qkc), b3(v_m), b3(o_m), b3(gates), gates_t, b_row, bias[:, None],
                                     P["g_mhead"][j].reshape(1, 512), c0, n0, m0, batch, cl, cfg["bb"])
            h_m = h_m.reshape(m, 512)
            if past is None:
                o_a = _dsa_prompt(qa, qi, gates_t, kab, vab, kib, batch, min(DSA_TOPK, tt // 4), tq_dsa)
            else:
                pt, n_pages, pp = past["pt"], past["n_pages"], cfg["pp"]
                pad_page = lambda a: jnp.pad(a.reshape(batch, tt, -1), ((0, 0), (0, PAGE - tt), (0, 0)))
                keys_past, keys_new = _dsa_scores_sample(pt, qi, gates, pad_page(kib), past["kidx"], j,
                                                         batch, tt, n_pages, cfg["pp_narrow"])
                topk = min(DSA_TOPK, (n_pages * PAGE + tt) // 4)
                thr, cut = _thresh_sample(keys_past, keys_new, topk, cfg["bg"])
                o_a = _dsa_attend_sample(pt, qa, thr, cut, keys_past, keys_new, pad_page(kab), pad_page(vab),
                                         past["k"], past["v"], j, batch, tt, n_pages, pp)
            x = _ab_out(x, h_m, o_a, W["w_out_ab"], j, tm)
            ab_states.append((c1, n1, m1.reshape(batch, M_HEADS),
                              qk_m.reshape(batch, tt, 512)[:, tt - (M_CONV - 1):],
                              ka.reshape(batch, tt, A_KV, A_DIM), va.reshape(batch, tt, A_KV, A_DIM),
                              ki.reshape(batch, tt, IDX_DIM)))
        else:
            ckv, kr, kcat, qcat = _mla_pre(x, g_attn, W["w_mla"], P["g_cq"][j][None, :],
                                           P["g_ckv"][j][None, :], W["w_uq"], W["w_uk"], j, tabs, tm)
            if past is None:
                ol = _mla_prompt(qcat, kcat, batch, tq_mla)
            else:
                pad_page = lambda a: jnp.pad(a.reshape(batch, tt, -1), ((0, 0), (0, PAGE - tt), (0, 0)))
                ol = _mla_sample(past["pt"], qcat, pad_page(kcat), past["ckv"], past["kr"], j,
                                 batch, tt, past["n_pages"], cfg["pp_narrow"])
            x = _mla_out(x, ol, W["w_uv"], W["w_out_mla"], j, tm)
            c_states.append((ckv.reshape(batch, tt, KV_LORA), kr.reshape(batch, tt, ROPE_D)))
        final = l == DEPTH - 1
        g_ffn = P["g_ffn"][l][None, :]
        ffn_w = (g_ffn, W["w_up"], P["w_fconv"][l], P["b_fconv"][l][None, :], W["w_down"], l)
        if past is None:
            res = _ffn(x, *ffn_w, jnp.zeros((batch, 8, D_FF), F32), P["g_final"][None, :], batch,
                       cfg["tm_ffn"], cfg["fc"], 1, final)
            ffn_states.append(res[1][:, -1, 8 - (FFN_CONV - 1):, :])
        else:
            res = _ffn(x.reshape(batch, tt, D_MODEL), *ffn_w, past["fconv"][l], P["g_final"][None, :], 1,
                       cfg["tm_ffn"], cfg["fc"], batch, final, xsteps=tt, bsteps=FFN_CONV - 1)
            ffn_states.append(res[1])
        x = res[0].reshape(m, D_MODEL)
        if final:
            y_final = res[2].reshape(batch, tt, D_MODEL)
    ab = [jnp.stack(s) for s in zip(*ab_states)]
    cc = [jnp.stack(s) for s in zip(*c_states)]
    return (y_final, *ab, *cc, jnp.stack(ffn_states))


def kernel(x_prompt, x_sample, state_mlstm_C, state_mlstm_n, state_mlstm_m, state_mlstm_conv, cache_dsa_k, cache_dsa_v, cache_dsa_kidx, cache_mla_ckv, cache_mla_krope, state_ffn_conv, page_table, g_attn, g_ffn, g_final, w_in_ab, w_mconv, b_igate, b_fgate, g_mhead, w_out_ab, w_in_mla, g_cq, g_ckv, w_uq, w_uk, w_uv, w_out_mla, w_up, w_fconv, b_fconv, w_down):
    W = _prep_weights(w_in_ab, w_out_ab, w_in_mla, w_uq, w_uk, w_uv, w_out_mla, w_up, w_down)
    P = dict(g_attn=g_attn, g_ffn=g_ffn, g_final=g_final, w_mconv=w_mconv, b_igate=b_igate, b_fgate=b_fgate,
             g_mhead=g_mhead, g_cq=g_cq, g_ckv=g_ckv, w_fconv=w_fconv, b_fconv=b_fconv)
    bp, tp, _ = x_prompt.shape
    bs, ts, _ = x_sample.shape
    n_pages = page_table.shape[1]
    n_pool = cache_dsa_k.shape[1]

    cfg_p = dict(batch=bp, t=tp, tm=min(512, tp), tq_dsa=min(256, tp), tq_mla=min(256, tp),
                 tm_ffn=min(512, tp), fc=1408, bb=1)
    out_p = _trunk(x_prompt.reshape(bp * tp, D_MODEL), jnp.arange(tp), P, W, None, cfg_p)

    past = dict(C=state_mlstm_C, n=state_mlstm_n, m=state_mlstm_m, mconv=state_mlstm_conv,
                k=cache_dsa_k.reshape(cache_dsa_k.shape[0], n_pool, PAGE * A_KV, A_DIM),
                v=cache_dsa_v.reshape(cache_dsa_v.shape[0], n_pool, PAGE * A_KV, A_DIM),
                kidx=jnp.swapaxes(cache_dsa_kidx, 2, 3), ckv=cache_mla_ckv,
                kr=jnp.swapaxes(cache_mla_krope, 2, 3), fconv=state_ffn_conv,
                pt=page_table.reshape(-1), n_pages=n_pages)
    ms = bs * ts
    pos_s = n_pages * PAGE + jnp.tile(jnp.arange(ts), bs)
    cfg_s = dict(batch=bs, t=ts, tm=min(512, ms), tq_dsa=None, tq_mla=None, tm_ffn=ms, fc=256,
                 pp=min(32, n_pages), pp_narrow=min(64, n_pages), bg=min(32, bs), bb=1)
    out_s = _trunk(x_sample.reshape(ms, D_MODEL), pos_s, P, W, past, cfg_s)
    return (out_p[0], out_s[0], *out_p[1:], *out_s[1:])
```

```python
import functools
import math

import jax
import jax.numpy as jnp
import numpy as np
from jax import lax
from jax.experimental import pallas as pl
from jax.experimental.pallas import tpu as pltpu

F32 = jnp.float32
BF16 = jnp.bfloat16
I32 = jnp.int32

D_MODEL = 1024
DEPTH = 4
PAGE = 128
M_HEADS, M_QK, M_V, M_CONV, M_CHUNK = 4, 64, 128, 4, 128
A_HEADS, A_KV, A_DIM = 4, 2, 128
IDX_HEADS, IDX_DIM, DSA_TOPK = 4, 64, 256
IDX_SCALE = (IDX_HEADS * IDX_DIM) ** -0.5
C_HEADS, Q_LORA, KV_LORA, NOPE, ROPE_D, C_V = 8, 256, 128, 128, 64, 128
MLA_SCALE = (NOPE + ROPE_D) ** -0.5
D_FF, FFN_CONV = 2816, 3
ROPE_THETA = 10000.0
EPS = 1e-6
AB_WIDTHS = (512, 512, 512, 4, 4, 512, 256, 256, 256, 64, 4)
AB_PACKED = 2944
INT_MIN = -(2 ** 31)
NEG = -1e30
LOG2E = 1.4426950408889634
VMEM_LIMIT = 56 * 1024 * 1024

_NT = (((1,), (1,)), ((), ()))


def _cp(sem):
    return pltpu.CompilerParams(dimension_semantics=sem, vmem_limit_bytes=VMEM_LIMIT)


def _dot(a, b):
    return jnp.dot(a, b, preferred_element_type=F32)


def _dot_nt(a, b):
    return lax.dot_general(a, b, _NT, preferred_element_type=F32)


def _rms(x, g):
    return x * lax.rsqrt(jnp.mean(x * x, axis=-1, keepdims=True) + EPS) * g


def _rope_piece(p, cos, sin, d):
    if d == 128:
        rot = pltpu.roll(p, 64, axis=1)
    else:
        lane = lax.broadcasted_iota(I32, p.shape, 1)
        rot = jnp.where((lane & 63) < 32, pltpu.roll(p, 96, axis=1), pltpu.roll(p, 32, axis=1))
    return p * cos + rot * sin


def _rope_wide(x, cos, sin, d):
    n = x.shape[1] // 128
    return jnp.concatenate([_rope_piece(x[:, i * 128:(i + 1) * 128], cos, sin, d) for i in range(n)], axis=1)


KEY_OF_NEG_INF = -2139095041


def _key_to_score(key):
    bits = key ^ ((key >> 31) & 0x7FFFFFFF)
    return jnp.where(key >= KEY_OF_NEG_INF, pltpu.bitcast(bits, F32), -jnp.inf)


def _lane_fold(x, op):
    out = x[:, 0:128]
    for u in range(1, x.shape[1] // 128):
        out = op(out, x[:, u * 128:(u + 1) * 128])
    return out


def _topk_threshold(count, vec_shape, k, idx_bits, key_axis):
    sl = lambda a, rs: a if rs is None else a[rs]
    t0 = jnp.where(count(lambda sc, i0, rs: sc >= 0.0) >= k, 0, INT_MIN).astype(I32)

    def bit_body(i, t):
        cand_key = t | jnp.left_shift(jnp.int32(1), 30 - i)
        cand = _key_to_score(cand_key)
        return jnp.where(count(lambda sc, i0, rs: sc >= sl(cand, rs)) >= k, cand_key, t)

    thr = _key_to_score(lax.fori_loop(0, 31, bit_body, t0))
    n_gt = count(lambda sc, i0, rs: sc > sl(thr, rs))
    n_eq = count(lambda sc, i0, rs: sc == sl(thr, rs))
    need = k - n_gt
    tie = jnp.max(jnp.where((n_eq > need) & (thr > -jnp.inf), 1, 0)) > 0

    def search():
        def idx_body(i, p):
            cand = p | jnp.left_shift(jnp.int32(1), idx_bits - 1 - i)

            def pred(kc, i0, rs):
                idx = i0 + lax.broadcasted_iota(I32, kc.shape, key_axis)
                return (kc == sl(thr, rs)) & (idx < sl(cand, rs))
            return jnp.where(count(pred) < need, cand, p)
        return lax.fori_loop(0, idx_bits, idx_body, jnp.zeros(vec_shape, I32))

    cut = lax.cond(tie, search, lambda: jnp.full(vec_shape, 2 ** 31 - 1, I32))
    return thr, cut


def _selected(sc, idx, thr, cut):
    return (sc > thr) | ((sc == thr) & (idx <= cut) & (thr > -jnp.inf))


def _flash_init(m_ref, l_ref, acc_ref):
    m_ref[...] = jnp.full(m_ref.shape, NEG, F32)
    l_ref[...] = jnp.zeros(l_ref.shape, F32)
    acc_ref[...] = jnp.zeros(acc_ref.shape, F32)


def _flash_update(s2, v, m_ref, l_ref, acc_ref, keep=None):
    if keep is not None:
        s2 = jnp.where(keep, s2, NEG)
    m_old = m_ref[...]
    m_new = jnp.maximum(m_old, jnp.max(_lane_fold(s2, jnp.maximum), axis=1, keepdims=True))
    alpha = jnp.exp2(m_old - m_new)
    n = s2.shape[1] // 128
    tiles = [jnp.exp2(s2[:, u * 128:(u + 1) * 128] - m_new) for u in range(n)]
    if keep is not None:
        tiles = [jnp.where(keep[:, u * 128:(u + 1) * 128], t, 0.0) for u, t in enumerate(tiles)]
    lsum = tiles[0]
    for t in tiles[1:]:
        lsum = lsum + t
    l_ref[...] = alpha * l_ref[...] + lsum
    p = jnp.concatenate(tiles, axis=1).astype(BF16)
    acc_ref[...] = alpha * acc_ref[...] + _dot(p, v)
    m_ref[...] = m_new


def _flash_finish(l_ref, acc_ref):
    return acc_ref[...] / jnp.sum(l_ref[...], axis=1, keepdims=True)


def _ab_proj_kernel(x_ref, g_ref, w_ref, c128_ref, s128_ref, c64_ref, s64_ref,
                    qk_ref, v_ref, o_ref, qa_ref, ka_ref, kab_ref, va_ref, vab_ref,
                    qi_ref, ki_ref, kib_ref, gt_ref):
    h = _rms(x_ref[...], g_ref[...]).astype(BF16)

    def seg(a, b):
        return _dot(h, w_ref[:, a:b])

    qk_ref[...] = seg(0, 512)
    v_ref[...] = seg(512, 1024)
    o_ref[...] = seg(1024, 1536)
    c128, s128, c64, s64 = c128_ref[...], s128_ref[...], c64_ref[...], s64_ref[...]
    qa_ref[...] = _rope_wide(seg(1536, 2048), c128, s128, 128).astype(BF16)
    tm = x_ref.shape[0]
    ka = _rope_wide(seg(2048, 2304), c128, s128, 128)
    kab_ref[...] = ka.astype(BF16)
    va = seg(2304, 2560)
    vab_ref[...] = va.astype(BF16)
    for gi in range(A_KV):
        ka_ref[pl.ds(gi, tm, stride=A_KV), :] = ka[:, gi * A_DIM:(gi + 1) * A_DIM]
        va_ref[pl.ds(gi, tm, stride=A_KV), :] = va[:, gi * A_DIM:(gi + 1) * A_DIM]
    qi_ref[...] = _rope_wide(seg(2560, 2816), c64, s64, 64).astype(BF16)
    last = seg(2816, 2944)
    ki = _rope_piece(last, c64, s64, 64)[:, :64]
    ki_ref[...] = ki
    kib_ref[...] = ki.astype(BF16)
    gt_ref[...] = last[:, 64:]


def _layer_spec(w, layer):
    zeros = (0,) * (w.ndim - 1)
    return pl.BlockSpec((None,) + w.shape[1:], lambda *_: (layer,) + zeros)


def _ab_proj(x, g, w, layer, tabs, tm):
    m = x.shape[0]
    c128, s128, c64, s64 = tabs
    tb = c128.shape[0] // tm
    row = lambda n: pl.BlockSpec((tm, n), lambda i: (i, 0))
    tab = pl.BlockSpec((tm, 128), lambda i: (i % tb, 0))
    full = lambda a: pl.BlockSpec(a.shape, lambda i: (0,) * a.ndim)
    widths = (512, 512, 512, 512, 256, 256, 256, 256, 256, 64, 64, 64)
    dtypes = (F32, F32, F32, BF16, F32, BF16, F32, BF16, BF16, F32, BF16, F32)
    out_specs = [row(n) for n in widths]
    out_shape = [jax.ShapeDtypeStruct((m, n), dt) for n, dt in zip(widths, dtypes)]
    for o in (4, 6):
        out_specs[o] = pl.BlockSpec((A_KV * tm, A_DIM), lambda i: (i, 0))
        out_shape[o] = jax.ShapeDtypeStruct((A_KV * m, A_DIM), F32)
    return pl.pallas_call(
        _ab_proj_kernel,
        grid=(m // tm,),
        in_specs=[row(D_MODEL), full(g), _layer_spec(w, layer), tab, tab, tab, tab],
        out_specs=out_specs,
        out_shape=out_shape,
        compiler_params=_cp(("parallel",)),
    )(x, g, w, c128, s128, c64, s64)


def _conv_silu_kernel(x_ref, buf_ref, w_ref, sc_ref, o_ref, work, *, tm, hb, stride, taps):
    @pl.when(pl.program_id(1) == 0)
    def _():
        work[0:hb, :] = buf_ref[0]

    x = x_ref[...]
    work[hb:hb + tm, :] = x
    y = x * w_ref[taps - 1:taps, :]
    for j in range(taps - 1):
        y = y + work[pl.ds(hb - (taps - 1 - j) * stride, tm), :] * w_ref[j:j + 1, :]
    o_ref[...] = y * jax.nn.sigmoid(y) * sc_ref[...]
    work[0:hb, :] = x[tm - hb:, :]


def _conv_silu(x, bufp, w, scale, groups, tm, stride):
    m, c = x.shape
    hb = bufp.shape[1]
    nt = m // groups // tm
    taps = w.shape[0]
    return pl.pallas_call(
        functools.partial(_conv_silu_kernel, tm=tm, hb=hb, stride=stride, taps=taps),
        grid=(groups, nt),
        in_specs=[pl.BlockSpec((tm, c), lambda g, i: (g * nt + i, 0)),
                  pl.BlockSpec((1, hb, c), lambda g, i: (g, 0, 0)),
                  pl.BlockSpec(w.shape, lambda g, i: (0, 0)),
                  pl.BlockSpec(scale.shape, lambda g, i: (0, 0))],
        out_specs=pl.BlockSpec((tm, c), lambda g, i: (g * nt + i, 0)),
        out_shape=jax.ShapeDtypeStruct((m, c), F32),
        scratch_shapes=[pltpu.VMEM((hb + tm, c), F32)],
        compiler_params=_cp(("arbitrary", "arbitrary")),
    )(x, bufp, w, scale)


def _log_sigmoid(x):
    return jnp.minimum(x, 0.0) - jnp.log1p(jnp.exp(-jnp.abs(x)))


def _mlstm_kernel(qk_ref, v_ref, o_ref, gt_ref, gtt_ref, brow_ref, bcol_ref, gh_ref,
                  c0_ref, n0_ref, m0_ref, h_ref, c1_ref, n1_ref, m1_ref, c_s, n_s, m_s, *, cl, nc, bb):
    c = pl.program_id(1)

    @pl.when(c == 0)
    def _():
        for bi in range(bb):
            for h in range(M_HEADS):
                c_s[bi * M_HEADS + h] = c0_ref[bi, h]
                n_s[bi * M_HEADS + h] = n0_ref[bi, h:h + 1, :]
                m_s[bi * M_HEADS + h] = m0_ref[bi, :, h:h + 1]

    r_i = lax.broadcasted_iota(I32, (cl, cl), 0)
    c_i = lax.broadcasted_iota(I32, (cl, cl), 1)
    causal = c_i <= r_i
    for bi in range(bb):
        _mlstm_chunk(bi, qk_ref[bi], v_ref[bi], o_ref[bi], gt_ref[bi] + brow_ref[...],
                     gtt_ref[bi] + bcol_ref[...], gh_ref, h_ref, c_s, n_s, m_s, causal, r_i, c_i, cl)

    @pl.when(c == nc - 1)
    def _():
        for bi in range(bb):
            for h in range(M_HEADS):
                c1_ref[bi, h] = c_s[bi * M_HEADS + h]
                n1_ref[bi, h:h + 1, :] = n_s[bi * M_HEADS + h]
                m1_ref[bi, :, h:h + 1] = m_s[bi * M_HEADS + h]


def _mlstm_chunk(bi, qk, vv, og, gates, gates_t, gh_ref, h_ref, c_s, n_s, m_s, causal, r_i, c_i, cl):
    li_col = gates[:, 0:4]
    lf_col = _log_sigmoid(gates[:, 4:8])
    li_row = gates_t[0:4, :]
    lf_row = _log_sigmoid(gates_t[4:8, :])
    outs = []
    for h in range(M_HEADS):
        sh = bi * M_HEADS + h
        q = qk[:, h * M_QK:(h + 1) * M_QK]
        k = qk[:, 256 + h * M_QK:256 + (h + 1) * M_QK]
        v = vv[:, h * M_V:(h + 1) * M_V]
        cm = c_s[sh]
        n = n_s[sh]
        m_prev = m_s[sh]
        b_col = jnp.sum(jnp.where(causal, lf_row[h:h + 1, :], 0.0), axis=1, keepdims=True)
        b_row = jnp.sum(jnp.where(r_i <= c_i, lf_col[:, h:h + 1], 0.0), axis=0, keepdims=True)
        dmat = jnp.where(causal, b_col - b_row + li_row[h:h + 1, :], -jnp.inf)
        inter = b_col + m_prev
        m_t = jnp.maximum(inter, jnp.max(dmat, axis=1, keepdims=True))
        iw = jnp.exp(inter - m_t)
        qb, kb, vb = q.astype(BF16), k.astype(BF16), v.astype(BF16)
        s = _dot_nt(qb, kb) * jnp.exp(dmat - m_t)
        num = iw * _dot(qb, cm.astype(BF16)) + _dot(s.astype(BF16), vb)
        den = iw * jnp.sum(q * n, axis=1, keepdims=True) + jnp.sum(s, axis=1, keepdims=True)
        hh = num / jnp.maximum(jnp.abs(den), jnp.exp(-m_t))
        m_new = m_t[cl - 1:cl, :]
        b_last = b_col[cl - 1:cl, :]
        w_end = jnp.exp(b_last - b_col + li_col[:, h:h + 1] - m_new)
        decay = jnp.exp(b_last + m_prev - m_new)
        kw = k * w_end
        c_s[sh] = decay * cm + lax.dot_general(kw.astype(BF16), vb, (((0,), (0,)), ((), ())),
                                               preferred_element_type=F32)
        n_s[sh] = decay * n + jnp.sum(kw, axis=0, keepdims=True)
        m_s[sh] = m_new
        hn = _rms(hh, gh_ref[:, h * M_V:(h + 1) * M_V])
        outs.append(hn * jax.nn.sigmoid(og[:, h * M_V:(h + 1) * M_V]))
    h_ref[bi] = jnp.concatenate(outs, axis=1)


def _mlstm(qkc, v, o, gates, gates_t, b_row, b_col, g_head, c0, n0, m0, batch, cl, bb):
    t = qkc.shape[1]
    nc = t // cl
    row = lambda n: pl.BlockSpec((bb, cl, n), lambda b, c: (b, c, 0))
    full = lambda a: pl.BlockSpec(a.shape, lambda b, c: (0,) * a.ndim)
    st_c = pl.BlockSpec((bb, M_HEADS, M_QK, M_V), lambda b, c: (b, 0, 0, 0))
    st_n = pl.BlockSpec((bb, M_HEADS, M_QK), lambda b, c: (b, 0, 0))
    st_m = pl.BlockSpec((bb, 1, M_HEADS), lambda b, c: (b, 0, 0))
    return pl.pallas_call(
        functools.partial(_mlstm_kernel, cl=cl, nc=nc, bb=bb),
        grid=(batch // bb, nc),
        in_specs=[row(512), row(512), row(512), row(64),
                  pl.BlockSpec((bb, 8, cl), lambda b, c: (b, 0, c)),
                  full(b_row), full(b_col), full(g_head), st_c, st_n, st_m],
        out_specs=[row(512), st_c, st_n, st_m],
        out_shape=[jax.ShapeDtypeStruct((batch, t, 512), F32),
                   jax.ShapeDtypeStruct((batch, M_HEADS, M_QK, M_V), F32),
                   jax.ShapeDtypeStruct((batch, M_HEADS, M_QK), F32),
                   jax.ShapeDtypeStruct((batch, 1, M_HEADS), F32)],
        scratch_shapes=[pltpu.VMEM((bb * M_HEADS, M_QK, M_V), F32),
                        pltpu.VMEM((bb * M_HEADS, 1, M_QK), F32),
                        pltpu.VMEM((bb * M_HEADS, 1, 1), F32)],
        compiler_params=_cp(("arbitrary", "arbitrary")),
    )(qkc, v, o, gates, gates_t, b_row, b_col, g_head, c0, n0, m0)


def _dsa_prompt_kernel(qa_ref, qi_ref, gtt_ref, kab_ref, vab_ref, kib_ref, o_ref,
                       keys_t, q_s, m_s, l_s, acc_s, *, tq, cw, topk, idx_bits):
    i = pl.program_id(1)
    n_ch = (i * tq + tq + cw - 1) // cw
    qpos = i * tq + lax.broadcasted_iota(I32, (1, tq), 1)
    wi = gtt_ref[0][8:12, :] * IDX_SCALE
    qi = qi_ref[...]

    def score_body(c, carry):
        off = pl.multiple_of(c * cw, cw)
        kc = kib_ref[pl.ds(off, cw), :]
        sc = jnp.zeros((cw, tq), F32)
        for h in range(IDX_HEADS):
            rel = jnp.maximum(_dot_nt(kc, qi[:, h * IDX_DIM:(h + 1) * IDX_DIM]), 0.0)
            sc = sc + rel * wi[h:h + 1, :]
        kpos = off + lax.broadcasted_iota(I32, (cw, 1), 0)
        keys_t[c] = jnp.where(kpos <= qpos, sc, -jnp.inf)
        return carry

    lax.fori_loop(0, n_ch, score_body, 0)

    def count(pred):
        def body(c, acc):
            for r0 in range(0, cw, 128):
                p = pred(keys_t[c, r0:r0 + 128, :], c * cw + r0, None).astype(F32)
                acc = acc + jnp.sum(p.reshape(4, 32, tq), axis=0)
            return acc
        acc = lax.fori_loop(0, n_ch, body, jnp.zeros((32, tq), F32))
        return jnp.sum(acc, axis=0, keepdims=True)

    thr, cut = _topk_threshold(count, (1, tq), topk, idx_bits, key_axis=0)

    qa = qa_ref[...]
    for g in range(A_KV):
        q_s[g] = jnp.concatenate([qa[:, (2 * g) * A_DIM:(2 * g + 1) * A_DIM],
                                  qa[:, (2 * g + 1) * A_DIM:(2 * g + 2) * A_DIM]], axis=0)
        _flash_init(m_s.at[g], l_s.at[g], acc_s.at[g])

    def att_body(c, carry):
        off = pl.multiple_of(c * cw, cw)
        idx = off + lax.broadcasted_iota(I32, (cw, tq), 0)
        bias = jnp.where(_selected(keys_t[c], idx, thr, cut), 0.0, NEG).T
        bias2 = jnp.concatenate([bias, bias], axis=0)
        for g in range(A_KV):
            kg = kab_ref[pl.ds(off, cw), g * A_DIM:(g + 1) * A_DIM]
            vg = vab_ref[pl.ds(off, cw), g * A_DIM:(g + 1) * A_DIM]
            s2 = _dot_nt(q_s[g], kg) * (A_DIM ** -0.5 * LOG2E) + bias2
            _flash_update(s2, vg, m_s.at[g], l_s.at[g], acc_s.at[g])
        return carry

    lax.fori_loop(0, n_ch, att_body, 0)
    outs = []
    for g in range(A_KV):
        og = _flash_finish(l_s.at[g], acc_s.at[g])
        outs += [og[0:tq], og[tq:2 * tq]]
    o_ref[...] = jnp.concatenate(outs, axis=1)


def _dsa_prompt(qa, qi, gates_t, kab, vab, kib, batch, topk, tq):
    m = qa.shape[0]
    t = m // batch
    nq = t // tq
    cw = min(512, t)
    row = lambda n: pl.BlockSpec((tq, n), lambda b, i: (b * nq + i, 0))
    whole = lambda n: pl.BlockSpec((t, n), lambda b, i: (b, 0))
    return pl.pallas_call(
        functools.partial(_dsa_prompt_kernel, tq=tq, cw=cw, topk=topk,
                          idx_bits=max(1, (t - 1).bit_length())),
        grid=(batch, nq),
        in_specs=[row(512), row(256), pl.BlockSpec((1, 16, tq), lambda b, i: (b, 0, i)),
                  whole(256), whole(256), whole(64)],
        out_specs=row(512),
        out_shape=jax.ShapeDtypeStruct((m, 512), F32),
        scratch_shapes=[pltpu.VMEM((t // cw, cw, tq), F32),
                        pltpu.VMEM((A_KV, 2 * tq, A_DIM), BF16),
                        pltpu.VMEM((A_KV, 2 * tq, 128), F32),
                        pltpu.VMEM((A_KV, 2 * tq, 128), F32),
                        pltpu.VMEM((A_KV, 2 * tq, A_DIM), F32)],
        compiler_params=_cp(("parallel", "arbitrary")),
    )(qa, qi, gates_t, kab, vab, kib)


def _idx_scores(qi, gates, keys_mat, keys_on_lanes):
    qs = jnp.concatenate([qi[:, h * IDX_DIM:(h + 1) * IDX_DIM] for h in range(IDX_HEADS)], axis=0)
    ws = jnp.concatenate([gates[:, 8 + h:9 + h] for h in range(IDX_HEADS)], axis=0) * IDX_SCALE
    qk = _dot(qs, keys_mat) if keys_on_lanes else _dot_nt(qs, keys_mat)
    rel = jnp.maximum(qk, 0.0) * ws
    t = qi.shape[0]
    sc = rel[0:t]
    for h in range(1, IDX_HEADS):
        sc = sc + rel[h * t:(h + 1) * t]
    return sc


def _dsa_scores_sample_kernel(pt_ref, qi_ref, gt_ref, knew_ref, *rest, pp):
    pages = rest[:pp]
    kp_ref, kn_ref = rest[pp], rest[pp + 1]
    qi = qi_ref[...]
    gates = gt_ref[...]
    kcat = jnp.concatenate([p[...] for p in pages], axis=1).astype(BF16)
    kp_ref[0] = _idx_scores(qi, gates, kcat, True)
    sn = _idx_scores(qi, gates, knew_ref[0], False)
    t = qi.shape[0]
    vis = lax.broadcasted_iota(I32, (t, PAGE), 1) <= lax.broadcasted_iota(I32, (t, PAGE), 0)
    kn_ref[0] = jnp.where(vis, sn, -jnp.inf)


def _page_specs(shape_tail, layer, n_pages, pp):
    nd = len(shape_tail)

    def spec(u):
        return pl.BlockSpec((None, None) + shape_tail,
                            lambda b, j, pt: (layer, pt[b * n_pages + j * pp + u]) + (0,) * nd)
    return [spec(u) for u in range(pp)]


def _dsa_scores_sample(pt, qi, gates, knew, pool, layer, batch, tt, n_pages, pp):
    grid_spec = pltpu.PrefetchScalarGridSpec(
        num_scalar_prefetch=1,
        grid=(batch, n_pages // pp),
        in_specs=[pl.BlockSpec((tt, 256), lambda b, j, pt: (b, 0)),
                  pl.BlockSpec((tt, 64), lambda b, j, pt: (b, 0)),
                  pl.BlockSpec((1, PAGE, IDX_DIM), lambda b, j, pt: (b, 0, 0))]
                 + _page_specs((IDX_DIM, PAGE), layer, n_pages, pp),
        out_specs=[pl.BlockSpec((1, tt, pp * PAGE), lambda b, j, pt: (b, 0, j)),
                   pl.BlockSpec((1, tt, PAGE), lambda b, j, pt: (b, 0, 0))],
    )
    return pl.pallas_call(
        functools.partial(_dsa_scores_sample_kernel, pp=pp),
        grid_spec=grid_spec,
        out_shape=[jax.ShapeDtypeStruct((batch, tt, n_pages * PAGE), F32),
                   jax.ShapeDtypeStruct((batch, tt, PAGE), F32)],
        compiler_params=_cp(("parallel", "arbitrary")),
    )(pt, qi, gates, knew, *([pool] * pp))


def _thresh_sample_kernel(kp_ref, kn_ref, thr_ref, cut_ref, *, rows, cw, n_past, topk, idx_bits):
    def count(pred):
        tt = kp_ref.shape[1]
        rb = min(128, rows)
        parts = []
        for r0 in range(0, rows, rb):
            rs = slice(r0, r0 + rb)
            bs = slice(r0 // tt, (r0 + rb) // tt)
            acc = pred(kn_ref[bs].reshape(rb, PAGE), n_past * cw, rs).astype(F32)
            for c in range(n_past):
                kc = kp_ref[bs, :, c * cw:(c + 1) * cw].reshape(rb, cw)
                acc = acc + _lane_fold(pred(kc, c * cw, rs).astype(F32), jnp.add)
            parts.append(acc)
        return jnp.sum(jnp.concatenate(parts, axis=0), axis=1, keepdims=True)

    thr, cut = _topk_threshold(count, (rows, 1), topk, idx_bits, key_axis=1)
    thr_ref[...] = jnp.broadcast_to(thr, (rows, PAGE)).reshape(thr_ref.shape)
    cut_ref[...] = jnp.broadcast_to(cut, (rows, PAGE)).reshape(cut_ref.shape)


def _thresh_sample(keys_past, keys_new, topk, bg):
    batch, tt, lp = keys_past.shape
    cw = 512
    rows = bg * tt
    n_past = lp // cw
    blk = lambda n: pl.BlockSpec((bg, tt, n), lambda i: (i, 0, 0))
    return pl.pallas_call(
        functools.partial(_thresh_sample_kernel, rows=rows, cw=cw, n_past=n_past, topk=topk,
                          idx_bits=(lp + PAGE - 1).bit_length()),
        grid=(batch // bg,),
        in_specs=[blk(lp), blk(PAGE)],
        out_specs=[blk(PAGE), blk(PAGE)],
        out_shape=[jax.ShapeDtypeStruct((batch, tt, PAGE), F32), jax.ShapeDtypeStruct((batch, tt, PAGE), I32)],
        compiler_params=_cp(("parallel",)),
    )(keys_past, keys_new)


def _dsa_attend_sample_kernel(pt_ref, qa_ref, thr_ref, cut_ref, kp_ref, kn_ref, knew_ref, vnew_ref, *rest,
                              pp, n_steps, n_past_keys):
    kpages, vpages = rest[:pp], rest[pp:2 * pp]
    o_ref = rest[2 * pp]
    m_s, l_s, acc_s = rest[2 * pp + 1:]
    j = pl.program_id(1)
    tt = qa_ref.shape[0]

    @pl.when(j == 0)
    def _():
        for g in range(A_KV):
            _flash_init(m_s.at[g], l_s.at[g], acc_s.at[g])

    qa = qa_ref[...]
    qg = [jnp.concatenate([qa[:, (2 * g) * A_DIM:(2 * g + 1) * A_DIM],
                           qa[:, (2 * g + 1) * A_DIM:(2 * g + 2) * A_DIM]], axis=0) for g in range(A_KV)]
    thr = thr_ref[0][:, 0:1]
    cut = cut_ref[0][:, 0:1]

    def attend(kc, first_idx, kv_of_group):
        idx = first_idx + lax.broadcasted_iota(I32, kc.shape, 1)
        bias = jnp.where(_selected(kc, idx, thr, cut), 0.0, NEG)
        bias2 = jnp.concatenate([bias, bias], axis=0)
        for g in range(A_KV):
            kg, vg = kv_of_group(g)
            s2 = _dot_nt(qg[g], kg) * (A_DIM ** -0.5 * LOG2E) + bias2
            _flash_update(s2, vg, m_s.at[g], l_s.at[g], acc_s.at[g])

    def paged(g):
        rows = lambda p: p[pl.ds(g, PAGE, stride=A_KV), :]
        return (jnp.concatenate([rows(p) for p in kpages], axis=0).astype(BF16),
                jnp.concatenate([rows(p) for p in vpages], axis=0).astype(BF16))

    attend(kp_ref[0], j * (pp * PAGE), paged)

    @pl.when(j == n_steps - 1)
    def _():
        attend(kn_ref[0], n_past_keys,
               lambda g: (knew_ref[0][:, g * A_DIM:(g + 1) * A_DIM], vnew_ref[0][:, g * A_DIM:(g + 1) * A_DIM]))
        outs = []
        for g in range(A_KV):
            og = _flash_finish(l_s.at[g], acc_s.at[g])
            outs += [og[0:tt], og[tt:2 * tt]]
        o_ref[...] = jnp.concatenate(outs, axis=1)


def _dsa_attend_sample(pt, qa, thr, cut, keys_past, keys_new, knew, vnew, pool_k, pool_v,
                       layer, batch, tt, n_pages, pp):
    n_steps = n_pages // pp
    kvw = A_KV * A_DIM
    grid_spec = pltpu.PrefetchScalarGridSpec(
        num_scalar_prefetch=1,
        grid=(batch, n_steps),
        in_specs=[pl.BlockSpec((tt, 512), lambda b, j, pt: (b, 0)),
                  pl.BlockSpec((1, tt, PAGE), lambda b, j, pt: (b, 0, 0)),
                  pl.BlockSpec((1, tt, PAGE), lambda b, j, pt: (b, 0, 0)),
                  pl.BlockSpec((1, tt, pp * PAGE), lambda b, j, pt: (b, 0, j)),
                  pl.BlockSpec((1, tt, PAGE), lambda b, j, pt: (b, 0, 0)),
                  pl.BlockSpec((1, PAGE, kvw), lambda b, j, pt: (b, 0, 0)),
                  pl.BlockSpec((1, PAGE, kvw), lambda b, j, pt: (b, 0, 0))]
                 + _page_specs((PAGE * A_KV, A_DIM), layer, n_pages, pp)
                 + _page_specs((PAGE * A_KV, A_DIM), layer, n_pages, pp),
        out_specs=pl.BlockSpec((tt, 512), lambda b, j, pt: (b, 0)),
        scratch_shapes=[pltpu.VMEM((A_KV, 2 * tt, 128), F32),
                        pltpu.VMEM((A_KV, 2 * tt, 128), F32),
                        pltpu.VMEM((A_KV, 2 * tt, A_DIM), F32)],
    )
    return pl.pallas_call(
        functools.partial(_dsa_attend_sample_kernel, pp=pp, n_steps=n_steps, n_past_keys=n_pages * PAGE),
        grid_spec=grid_spec,
        out_shape=jax.ShapeDtypeStruct((batch * tt, 512), F32),
        compiler_params=_cp(("parallel", "arbitrary")),
    )(pt, qa, thr, cut, keys_past, keys_new, knew, vnew, *([pool_k] * pp), *([pool_v] * pp))


def _mla_pre_kernel(x_ref, g_ref, win_ref, gq_ref, gkv_ref, wuq_ref, wuk_ref, c64_ref, s64_ref,
                    ckv_ref, kr_ref, kcat_ref, qcat_ref):
    h = _rms(x_ref[...], g_ref[...]).astype(BF16)
    cq = _rms(_dot(h, win_ref[:, 0:Q_LORA]), gq_ref[...])
    rest = _dot(h, win_ref[:, Q_LORA:Q_LORA + 256])
    ckv = _rms(rest[:, 0:KV_LORA], gkv_ref[...])
    ckv_ref[...] = ckv
    c64, s64 = c64_ref[...], s64_ref[...]
    lane = lax.broadcasted_iota(I32, (h.shape[0], 128), 1)
    kr = jnp.where(lane < ROPE_D, _rope_piece(rest[:, 128:256], c64, s64, 64), 0.0)
    kr_ref[...] = kr[:, :ROPE_D]
    kcat_ref[...] = jnp.concatenate([ckv, kr], axis=1).astype(BF16)
    q = _dot(cq.astype(BF16), wuq_ref[...])
    qr = _rope_wide(q[:, 1024:1536], c64, s64, 64)
    pieces = []
    for hh in range(C_HEADS):
        pieces.append(_dot(q[:, hh * NOPE:(hh + 1) * NOPE].astype(BF16), wuk_ref[hh]))
        pair = qr[:, (hh // 2) * 128:(hh // 2 + 1) * 128]
        if hh % 2:
            pair = pltpu.roll(pair, ROPE_D, axis=1)
        pieces.append(jnp.where(lane < ROPE_D, pair, 0.0))
    qcat_ref[...] = jnp.concatenate(pieces, axis=1).astype(BF16)


def _mla_pre(x, g, win, gq, gkv, wuq, wuk, layer, tabs, tm):
    m = x.shape[0]
    c64, s64 = tabs[2], tabs[3]
    tb = c64.shape[0] // tm
    row = lambda n: pl.BlockSpec((tm, n), lambda i: (i, 0))
    tab = pl.BlockSpec((tm, 128), lambda i: (i % tb, 0))
    full = lambda a: pl.BlockSpec(a.shape, lambda i: (0,) * a.ndim)
    widths = (KV_LORA, ROPE_D, 256, C_HEADS * 256)
    dtypes = (F32, F32, BF16, BF16)
    return pl.pallas_call(
        _mla_pre_kernel,
        grid=(m // tm,),
        in_specs=[row(D_MODEL), full(g), _layer_spec(win, layer), full(gq), full(gkv), _layer_spec(wuq, layer),
                  _layer_spec(wuk, layer), tab, tab],
        out_specs=[row(n) for n in widths],
        out_shape=[jax.ShapeDtypeStruct((m, n), dt) for n, dt in zip(widths, dtypes)],
        compiler_params=_cp(("parallel",)),
    )(x, g, win, gq, gkv, wuq, wuk, c64, s64)


def _stack_heads(qcat):
    return jnp.concatenate([qcat[:, h * 256:(h + 1) * 256] for h in range(C_HEADS)], axis=0)


def _unstack_heads(o, t):
    return jnp.concatenate([o[h * t:(h + 1) * t] for h in range(C_HEADS)], axis=1)


def _mla_prompt_kernel(q_ref, k_ref, o_ref, q_s, m_s, l_s, acc_s, *, tq):
    i = pl.program_id(1)
    q_s[...] = _stack_heads(q_ref[...])
    _flash_init(m_s, l_s, acc_s)

    def step(c, keep):
        kc = k_ref[pl.ds(pl.multiple_of(c * tq, tq), tq), :]
        s2 = _dot_nt(q_s[...], kc) * (MLA_SCALE * LOG2E)
        _flash_update(s2, kc[:, 0:KV_LORA], m_s, l_s, acc_s, keep=keep)

    def body(c, carry):
        step(c, None)
        return carry

    lax.fori_loop(0, i, body, 0)
    rows = C_HEADS * tq
    t_in = lax.broadcasted_iota(I32, (rows, tq), 0) & (tq - 1)
    step(i, lax.broadcasted_iota(I32, (rows, tq), 1) <= t_in)
    o_ref[...] = _unstack_heads(_flash_finish(l_s, acc_s), tq).astype(BF16)


def _mla_prompt(qcat, kcat, batch, tq):
    m = qcat.shape[0]
    t = m // batch
    nq = t // tq
    return pl.pallas_call(
        functools.partial(_mla_prompt_kernel, tq=tq),
        grid=(batch, nq),
        in_specs=[pl.BlockSpec((tq, C_HEADS * 256), lambda b, i: (b * nq + i, 0)),
                  pl.BlockSpec((t, 256), lambda b, i: (b, 0))],
        out_specs=pl.BlockSpec((tq, C_HEADS * KV_LORA), lambda b, i: (b * nq + i, 0)),
        out_shape=jax.ShapeDtypeStruct((m, C_HEADS * KV_LORA), BF16),
        scratch_shapes=[pltpu.VMEM((C_HEADS * tq, 256), BF16),
                        pltpu.VMEM((C_HEADS * tq, 128), F32),
                        pltpu.VMEM((C_HEADS * tq, 128), F32),
                        pltpu.VMEM((C_HEADS * tq, KV_LORA), F32)],
        compiler_params=_cp(("parallel", "arbitrary")),
    )(qcat, kcat)


def _mla_sample_kernel(pt_ref, q_ref, knew_ref, *rest, pp, n_steps):
    cpages, rpages = rest[:pp], rest[pp:2 * pp]
    o_ref = rest[2 * pp]
    m_s, l_s, acc_s = rest[2 * pp + 1:]
    j = pl.program_id(1)
    tt = q_ref.shape[0]

    @pl.when(j == 0)
    def _():
        _flash_init(m_s, l_s, acc_s)

    qs = _stack_heads(q_ref[...])
    ccat = jnp.concatenate([p[...] for p in cpages], axis=0).astype(BF16)
    rcat = jnp.concatenate([p[...] for p in rpages], axis=1).astype(BF16)
    s2 = (_dot_nt(qs[:, 0:KV_LORA], ccat) + _dot(qs[:, KV_LORA:KV_LORA + ROPE_D], rcat)) * (MLA_SCALE * LOG2E)
    _flash_update(s2, ccat, m_s, l_s, acc_s)

    @pl.when(j == n_steps - 1)
    def _():
        kn = knew_ref[0]
        rows = C_HEADS * tt
        t_in = lax.broadcasted_iota(I32, (rows, PAGE), 0) & (tt - 1)
        keep = lax.broadcasted_iota(I32, (rows, PAGE), 1) <= t_in
        _flash_update(_dot_nt(qs, kn) * (MLA_SCALE * LOG2E), kn[:, 0:KV_LORA], m_s, l_s, acc_s, keep=keep)
        o_ref[...] = _unstack_heads(_flash_finish(l_s, acc_s), tt).astype(BF16)


def _mla_sample(pt, qcat, knew, pool_c, pool_r, layer, batch, tt, n_pages, pp):
    n_steps = n_pages // pp
    grid_spec = pltpu.PrefetchScalarGridSpec(
        num_scalar_prefetch=1,
        grid=(batch, n_steps),
        in_specs=[pl.BlockSpec((tt, C_HEADS * 256), lambda b, j, pt: (b, 0)),
                  pl.BlockSpec((1, PAGE, 256), lambda b, j, pt: (b, 0, 0))]
                 + _page_specs((PAGE, KV_LORA), layer, n_pages, pp)
                 + _page_specs((ROPE_D, PAGE), layer, n_pages, pp),
        out_specs=pl.BlockSpec((tt, C_HEADS * KV_LORA), lambda b, j, pt: (b, 0)),
        scratch_shapes=[pltpu.VMEM((C_HEADS * tt, 128), F32),
                        pltpu.VMEM((C_HEADS * tt, 128), F32),
                        pltpu.VMEM((C_HEADS * tt, KV_LORA), F32)],
    )
    return pl.pallas_call(
        functools.partial(_mla_sample_kernel, pp=pp, n_steps=n_steps),
        grid_spec=grid_spec,
        out_shape=jax.ShapeDtypeStruct((batch * tt, C_HEADS * KV_LORA), BF16),
        compiler_params=_cp(("parallel", "arbitrary")),
    )(pt, qcat, knew, *([pool_c] * pp), *([pool_r] * pp))


def _ab_out_kernel(x_ref, hm_ref, oa_ref, w_ref, o_ref):
    o_ref[...] = (x_ref[...] + _dot(hm_ref[...].astype(BF16), w_ref[0:512, :])
                  + _dot(oa_ref[...].astype(BF16), w_ref[512:1024, :]))


def _ab_out(x, hm, oa, w, layer, tm):
    m = x.shape[0]
    row = lambda n: pl.BlockSpec((tm, n), lambda i: (i, 0))
    return pl.pallas_call(
        _ab_out_kernel,
        grid=(m // tm,),
        in_specs=[row(D_MODEL), row(512), row(512), _layer_spec(w, layer)],
        out_specs=row(D_MODEL),
        out_shape=jax.ShapeDtypeStruct((m, D_MODEL), F32),
        compiler_params=_cp(("parallel",)),
    )(x, hm, oa, w)


def _mla_out_kernel(x_ref, ol_ref, wuv_ref, w_ref, o_ref):
    ol = ol_ref[...]
    o = jnp.concatenate([_dot(ol[:, h * KV_LORA:(h + 1) * KV_LORA], wuv_ref[h]) for h in range(C_HEADS)],
                        axis=1)
    o_ref[...] = x_ref[...] + _dot(o.astype(BF16), w_ref[...])


def _mla_out(x, ol, wuv, w, layer, tm):
    m = x.shape[0]
    row = lambda n: pl.BlockSpec((tm, n), lambda i: (i, 0))
    return pl.pallas_call(
        _mla_out_kernel,
        grid=(m // tm,),
        in_specs=[row(D_MODEL), row(1024), _layer_spec(wuv, layer), _layer_spec(w, layer)],
        out_specs=row(D_MODEL),
        out_shape=jax.ShapeDtypeStruct((m, D_MODEL), F32),
        compiler_params=_cp(("parallel",)),
    )(x, ol, wuv, w)


def _load_tm(ref, lead, steps):
    if steps <= 1:
        return ref[lead + (slice(None), slice(None))]
    return jnp.concatenate([ref[:, s, :] for s in range(steps)], axis=0)


def _store_tm(ref, lead, val, steps):
    if steps <= 1:
        ref[lead + (slice(None), slice(None))] = val
        return
    nb = ref.shape[0]
    for s in range(steps):
        ref[:, s, :] = val[s * nb:(s + 1) * nb]


def _ffn_kernel(x_ref, g_ref, wa_ref, wg_ref, wc_ref, bc_ref, wd_ref, buf_ref, gf_ref,
                o_ref, st_ref, y_ref, h_s, acc_s, halo, work, *, tm, hb, stride, nj, final, xsteps, bsteps):
    i = pl.program_id(1)
    j = pl.program_id(2)

    @pl.when(j == 0)
    def _():
        h_s[...] = _rms(_load_tm(x_ref, (), xsteps), g_ref[...]).astype(BF16)
        acc_s[...] = jnp.zeros(acc_s.shape, F32)

    @pl.when(i == 0)
    def _():
        halo[j] = _load_tm(buf_ref, (0,), bsteps)

    h = h_s[...]
    a = _dot(h, wa_ref[...])
    gg = _dot(h, wg_ref[...])
    work[0:hb, :] = halo[j]
    work[hb:hb + tm, :] = gg
    gc = gg * wc_ref[FFN_CONV - 1:FFN_CONV, :]
    for t in range(FFN_CONV - 1):
        gc = gc + work[pl.ds(hb - (FFN_CONV - 1 - t) * stride, tm), :] * wc_ref[t:t + 1, :]
    last = gg[tm - hb:, :]
    halo[j] = last
    _store_tm(st_ref, (0, 0), last, bsteps)
    gc = gc + bc_ref[...]
    p = a * (gc * jax.nn.sigmoid(gc))
    acc_s[...] += _dot(p.astype(BF16), wd_ref[...])

    @pl.when(j == nj - 1)
    def _():
        xn = _load_tm(x_ref, (), xsteps) + acc_s[...]
        _store_tm(o_ref, (), xn, xsteps)
        if final:
            _store_tm(y_ref, (), _rms(xn, gf_ref[...]), xsteps)


def _ffn(x, g, w_up, wc, bc, wd, layer, bufp, g_final, groups, tm, fc, stride, final, xsteps=1, bsteps=1):
    nj = D_FF // fc
    vec = lambda a: pl.BlockSpec(a.shape, lambda gi, i, j: (0, 0))
    if xsteps > 1:
        nb = x.shape[0]
        assert groups == 1 and tm == nb * xsteps
        nt, hb = 1, nb * bsteps
        row = pl.BlockSpec((nb, xsteps, D_MODEL), lambda gi, i, j: (0, 0, 0))
        st = st_out = pl.BlockSpec((nb, bsteps, fc), lambda gi, i, j: (0, 0, j))
        st_shape = (nb, bsteps, D_FF)
    else:
        hb = bufp.shape[1]
        nt = x.shape[0] // groups // tm
        row = pl.BlockSpec((tm, D_MODEL), lambda gi, i, j: (gi * nt + i, 0))
        st = pl.BlockSpec((1, hb, fc), lambda gi, i, j: (gi, 0, j))
        st_out = pl.BlockSpec((1, 1, hb, fc), lambda gi, i, j: (gi, i, 0, j))
        st_shape = (groups, nt, hb, D_FF)
    outs = [row, st_out] + ([row] if final else [])
    shapes = [jax.ShapeDtypeStruct(x.shape, F32), jax.ShapeDtypeStruct(st_shape, F32)]
    if final:
        shapes.append(jax.ShapeDtypeStruct(x.shape, F32))

    static = dict(tm=tm, hb=hb, stride=stride, nj=nj, xsteps=xsteps, bsteps=bsteps)

    def kern(*refs):
        if final:
            return _ffn_kernel(*refs, final=True, **static)
        ins, rest = refs[:9], refs[9:]
        return _ffn_kernel(*ins, rest[0], rest[1], None, *rest[2:], final=False, **static)

    return pl.pallas_call(
        kern,
        grid=(groups, nt, nj),
        in_specs=[row, vec(g),
                  pl.BlockSpec((None, D_MODEL, fc), lambda gi, i, j: (layer, 0, j)),
                  pl.BlockSpec((None, D_MODEL, fc), lambda gi, i, j: (layer, 0, nj + j)),
                  pl.BlockSpec((FFN_CONV, fc), lambda gi, i, j: (0, j)),
                  pl.BlockSpec((1, fc), lambda gi, i, j: (0, j)),
                  pl.BlockSpec((None, fc, D_MODEL), lambda gi, i, j: (layer, j, 0)),
                  st, vec(g_final)],
        out_specs=outs,
        out_shape=shapes,
        scratch_shapes=[pltpu.VMEM((tm, D_MODEL), BF16),
                        pltpu.VMEM((tm, D_MODEL), F32),
                        pltpu.VMEM((nj, hb, fc), F32),
                        pltpu.VMEM((hb + tm, fc), F32)],
        compiler_params=_cp(("arbitrary", "arbitrary", "arbitrary")),
    )(x, g, w_up, w_up, wc, bc, wd, bufp, g_final)


def _rope_tables(pos, d):
    half = d // 2
    inv = ROPE_THETA ** (-jnp.arange(half, dtype=F32) * (2.0 / d))
    ang = pos.astype(F32)[:, None] * inv[None, :]
    cos, sin = jnp.cos(ang), jnp.sin(ang)
    reps = 128 // d
    return (jnp.tile(jnp.concatenate([cos, cos], axis=1), (1, reps)),
            jnp.tile(jnp.concatenate([-sin, sin], axis=1), (1, reps)))


def _prep_weights(w_in_ab, w_out_ab, w_in_mla, w_uq, w_uk, w_uv, w_out_mla, w_up, w_down):
    cuts = np.cumsum((0,) + AB_WIDTHS)
    seg = lambda w, k: w[:, :, cuts[k]:cuts[k + 1]]
    n_ab = w_in_ab.shape[0]
    pad = jnp.zeros((n_ab, D_MODEL, 52), F32)
    w_ab = jnp.concatenate([seg(w_in_ab, 0), seg(w_in_ab, 1), seg(w_in_ab, 2), seg(w_in_ab, 5),
                            seg(w_in_ab, 6), seg(w_in_ab, 7), seg(w_in_ab, 8), seg(w_in_ab, 9),
                            seg(w_in_ab, 3), seg(w_in_ab, 4), seg(w_in_ab, 10), pad], axis=2).astype(BF16)
    n_c = w_in_mla.shape[0]
    w_mla = jnp.concatenate([w_in_mla, jnp.zeros((n_c, D_MODEL, 64), F32)], axis=2).astype(BF16)
    uq = w_uq.reshape(n_c, Q_LORA, C_HEADS, NOPE + ROPE_D)
    w_uq2 = jnp.concatenate([uq[..., :NOPE].reshape(n_c, Q_LORA, C_HEADS * NOPE),
                             uq[..., NOPE:].reshape(n_c, Q_LORA, C_HEADS * ROPE_D)], axis=2).astype(BF16)
    return dict(w_ab=w_ab, w_out_ab=w_out_ab.astype(BF16), w_mla=w_mla, w_uq=w_uq2,
                w_uk=w_uk.astype(BF16), w_uv=w_uv.astype(BF16), w_out_mla=w_out_mla.astype(BF16),
                w_up=w_up.astype(BF16), w_down=w_down.astype(BF16))


def _front_pad(buf, hb):
    g, r, c = buf.shape
    return jnp.concatenate([jnp.zeros((g, hb - r, c), F32), buf], axis=1)


def _trunk(x, pos_rows, P, W, past, cfg):
    batch, tt = cfg["batch"], cfg["t"]
    tm, tq_dsa, tq_mla = cfg["tm"], cfg["tq_dsa"], cfg["tq_mla"]
    m = batch * tt
    c128, s128 = _rope_tables(pos_rows, 128)
    c64, s64 = _rope_tables(pos_rows, 64)
    tabs = (c128, s128, c64, s64)
    kscale = jnp.concatenate([jnp.ones((1, 256), F32), jnp.full((1, 256), M_QK ** -0.5, F32)], axis=1)
    ab_states, c_states, ffn_states = [], [], []
    y_final = None
    for l in range(DEPTH):
        j = l // 2
        g_attn = P["g_attn"][l][None, :]
        if l % 2 == 0:
            (qk_m, v_m, o_m, qa, ka, kab, va, vab, qi, ki, kib, gates) = _ab_proj(x, g_attn, W["w_ab"], j, tabs, tm)
            if past is None:
                conv_buf = jnp.zeros((batch, M_CONV - 1, 512), F32)
                c0 = jnp.zeros((batch, M_HEADS, M_QK, M_V), F32)
                n0 = jnp.zeros((batch, M_HEADS, M_QK), F32)
                m0 = jnp.zeros((batch, 1, M_HEADS), F32)
            else:
                conv_buf = past["mconv"][j]
                c0, n0, m0 = past["C"][j], past["n"][j], past["m"][j][:, None, :]
            qkc = _conv_silu(qk_m, _front_pad(conv_buf, 8), P["w_mconv"][j], kscale,
                             groups=batch, tm=min(tm, tt), stride=1)
            cl = math.gcd(tt, M_CHUNK)
            gates_t = jnp.transpose(gates[:, :16].reshape(batch, tt, 16), (0, 2, 1))
            bias = jnp.concatenate([P["b_igate"][j], P["b_fgate"][j]])
            b_row = jnp.concatenate([bias, jnp.zeros((56,), F32)])[None, :]
            b3 = lambda a: a.reshape(batch, tt, a.shape[-1])
            h_m, c1, n1, m1 = _mlstm(b3(qkc), b3(v_m), b3(o_m), b3(gates), gates_t, b_row, bias[:, None],
                                     P["g_mhead"][j].reshape(1, 512), c0, n0, m0, batch, cl, cfg["bb"])
            h_m = h_m.reshape(m, 512)
            if past is None:
                o_a = _dsa_prompt(qa, qi, gates_t, kab, vab, kib, batch, min(DSA_TOPK, tt // 4), tq_dsa)
            else:
                pt, n_pages, pp = past["pt"], past["n_pages"], cfg["pp"]
                pad_page = lambda a: jnp.pad(a.reshape(batch, tt, -1), ((0, 0), (0, PAGE - tt), (0, 0)))
                keys_past, keys_new = _dsa_scores_sample(pt, qi, gates, pad_page(kib), past["kidx"], j,
                                                         batch, tt, n_pages, cfg["pp_narrow"])
                topk = min(DSA_TOPK, (n_pages * PAGE + tt) // 4)
                thr, cut = _thresh_sample(keys_past, keys_new, topk, cfg["bg"])
                o_a = _dsa_attend_sample(pt, qa, thr, cut, keys_past, keys_new, pad_page(kab), pad_page(vab),
                                         past["k"], past["v"], j, batch, tt, n_pages, pp)
            x = _ab_out(x, h_m, o_a, W["w_out_ab"], j, tm)
            ab_states.append((c1, n1, m1.reshape(batch, M_HEADS),
                              qk_m.reshape(batch, tt, 512)[:, tt - (M_CONV - 1):],
                              ka.reshape(batch, tt, A_KV, A_DIM), va.reshape(batch, tt, A_KV, A_DIM),
                              ki.reshape(batch, tt, IDX_DIM)))
        else:
            ckv, kr, kcat, qcat = _mla_pre(x, g_attn, W["w_mla"], P["g_cq"][j][None, :],
                                           P["g_ckv"][j][None, :], W["w_uq"], W["w_uk"], j, tabs, tm)
            if past is None:
                ol = _mla_prompt(qcat, kcat, batch, tq_mla)
            else:
                pad_page = lambda a: jnp.pad(a.reshape(batch, tt, -1), ((0, 0), (0, PAGE - tt), (0, 0)))
                ol = _mla_sample(past["pt"], qcat, pad_page(kcat), past["ckv"], past["kr"], j,
                                 batch, tt, past["n_pages"], cfg["pp_narrow"])
            x = _mla_out(x, ol, W["w_uv"], W["w_out_mla"], j, tm)
            c_states.append((ckv.reshape(batch, tt, KV_LORA), kr.reshape(batch, tt, ROPE_D)))
        final = l == DEPTH - 1
        g_ffn = P["g_ffn"][l][None, :]
        ffn_w = (g_ffn, W["w_up"], P["w_fconv"][l], P["b_fconv"][l][None, :], W["w_down"], l)
        if past is None:
            res = _ffn(x, *ffn_w, jnp.zeros((batch, 8, D_FF), F32), P["g_final"][None, :], batch,
                       cfg["tm_ffn"], cfg["fc"], 1, final)
            ffn_states.append(res[1][:, -1, 8 - (FFN_CONV - 1):, :])
        else:
            res = _ffn(x.reshape(batch, tt, D_MODEL), *ffn_w, past["fconv"][l], P["g_final"][None, :], 1,
                       cfg["tm_ffn"], cfg["fc"], batch, final, xsteps=tt, bsteps=FFN_CONV - 1)
            ffn_states.append(res[1])
        x = res[0].reshape(m, D_MODEL)
        if final:
            y_final = res[2].reshape(batch, tt, D_MODEL)
    ab = [jnp.stack(s) for s in zip(*ab_states)]
    cc = [jnp.stack(s) for s in zip(*c_states)]
    return (y_final, *ab, *cc, jnp.stack(ffn_states))


def kernel(x_prompt, x_sample, state_mlstm_C, state_mlstm_n, state_mlstm_m, state_mlstm_conv, cache_dsa_k, cache_dsa_v, cache_dsa_kidx, cache_mla_ckv, cache_mla_krope, state_ffn_conv, page_table, g_attn, g_ffn, g_final, w_in_ab, w_mconv, b_igate, b_fgate, g_mhead, w_out_ab, w_in_mla, g_cq, g_ckv, w_uq, w_uk, w_uv, w_out_mla, w_up, w_fconv, b_fconv, w_down):
    W = _prep_weights(w_in_ab, w_out_ab, w_in_mla, w_uq, w_uk, w_uv, w_out_mla, w_up, w_down)
    P = dict(g_attn=g_attn, g_ffn=g_ffn, g_final=g_final, w_mconv=w_mconv, b_igate=b_igate, b_fgate=b_fgate,
             g_mhead=g_mhead, g_cq=g_cq, g_ckv=g_ckv, w_fconv=w_fconv, b_fconv=b_fconv)
    bp, tp, _ = x_prompt.shape
    bs, ts, _ = x_sample.shape
    n_pages = page_table.shape[1]
    n_pool = cache_dsa_k.shape[1]

    cfg_p = dict(batch=bp, t=tp, tm=min(512, tp), tq_dsa=min(256, tp), tq_mla=min(256, tp),
                 tm_ffn=min(512, tp), fc=1408, bb=1)
    out_p = _trunk(x_prompt.reshape(bp * tp, D_MODEL), jnp.arange(tp), P, W, None, cfg_p)

    past = dict(C=state_mlstm_C, n=state_mlstm_n, m=state_mlstm_m, mconv=state_mlstm_conv,
                k=cache_dsa_k.reshape(cache_dsa_k.shape[0], n_pool, PAGE * A_KV, A_DIM),
                v=cache_dsa_v.reshape(cache_dsa_v.shape[0], n_pool, PAGE * A_KV, A_DIM),
                kidx=jnp.swapaxes(cache_dsa_kidx, 2, 3), ckv=cache_mla_ckv,
                kr=jnp.swapaxes(cache_mla_krope, 2, 3), fconv=state_ffn_conv,
                pt=page_table.reshape(-1), n_pages=n_pages)
    ms = bs * ts
    pos_s = n_pages * PAGE + jnp.tile(jnp.arange(ts), bs)
    cfg_s = dict(batch=bs, t=ts, tm=min(512, ms), tq_dsa=None, tq_mla=None, tm_ffn=ms, fc=256,
                 pp=min(32, n_pages), pp_narrow=min(64, n_pages), bg=min(32, bs), bb=1)
    out_s = _trunk(x_sample.reshape(ms, D_MODEL), pos_s, P, W, past, cfg_s)
    return (out_p[0], out_s[0], *out_p[1:], *out_s[1:])
```
